```python
import jax, jax.numpy as jnp
from jax import lax
import numpy as np

D_MODEL = 4096
BATCH = 32
SEQ = 256
DEPTH = 1
DEC_BATCH = 8
DEC_SEQ = 2048
PAST_LEN = 512

GRID_W = 64
N_HEADS = 32
N_KV_HEADS = 8
HEAD_DIM = D_MODEL // N_HEADS
Q_PER_KV = N_HEADS // N_KV_HEADS
WINDOW = 128
BLOCK = 128
CONV_DIM = D_MODEL // 2
CONV_K = 3
D_FF = 256 * ((8 * D_MODEL // 3 + 255) // 256)
ROPE_THETA = 10000.0
EPS = 1e-6
N_MOD = 6
Q_W = N_HEADS * HEAD_DIM
KV_W = N_KV_HEADS * HEAD_DIM
IN_W = Q_W + 2 * KV_W + 3 * CONV_DIM + 2 * D_MODEL
NEG = -1e30

kernel_name = 'hybrid_dit_swa_shortconv_step'


def _rmsnorm(x, g):
    xf = x.astype(jnp.float32)
    y = xf * lax.rsqrt(jnp.mean(xf * xf, axis=-1, keepdims=True) + EPS)
    return (y * g.astype(jnp.float32)).astype(x.dtype)


def _modulation(cond, w_mod, b_mod):
    m = jax.nn.silu(cond) @ w_mod + b_mod
    return m.reshape(cond.shape[:-1] + (N_MOD, D_MODEL))


def _dwconv3(x, w):
    L = x.shape[1]
    xp = jnp.pad(x, ((0, 0), (1, 1), (0, 0)))
    return xp[:, :L] * w[0] + xp[:, 1:L + 1] * w[1] + xp[:, 2:] * w[2]


def _axial_rope_tables(L):
    rows = L // GRID_W
    row = jnp.repeat(jnp.arange(rows), GRID_W).astype(jnp.float32)
    col = jnp.tile(jnp.arange(GRID_W), rows).astype(jnp.float32)
    n_freq = HEAD_DIM // 4
    inv = ROPE_THETA ** (-jnp.arange(n_freq, dtype=jnp.float32) / n_freq)
    ang = jnp.concatenate([row[:, None] * inv, col[:, None] * inv], axis=-1)
    return jnp.cos(ang), jnp.sin(ang)


def _apply_axial_rope(x, cos, sin):
    B, L, H, _ = x.shape
    nf = HEAD_DIM // 4
    xf = x.astype(jnp.float32).reshape(B, L, H, 2, 2, nf)
    x1, x2 = xf[..., 0, :], xf[..., 1, :]
    c = cos.reshape(L, 1, 2, nf)
    s = sin.reshape(L, 1, 2, nf)
    out = jnp.stack([x1 * c - x2 * s, x1 * s + x2 * c], axis=-2)
    return out.reshape(B, L, H, HEAD_DIM).astype(x.dtype)


def _sink_attend(q_blk, k, v, sink, mask=None):
    B, Q = q_blk.shape[:2]
    qg = q_blk.reshape(B, Q, N_KV_HEADS, Q_PER_KV, HEAD_DIM)
    s = jnp.einsum('bqkgd,btkd->bkgqt', qg, k, preferred_element_type=jnp.float32) * (HEAD_DIM ** -0.5)
    if mask is not None:
        s = jnp.where(mask, s, NEG)
    sk = sink.astype(jnp.float32).reshape(1, N_KV_HEADS, Q_PER_KV, 1, 1)
    m = jnp.maximum(jnp.max(s, axis=-1, keepdims=True), sk)
    p = jnp.exp(s - m)
    p = (p / (jnp.sum(p, axis=-1, keepdims=True) + jnp.exp(sk - m))).astype(v.dtype)
    o = jnp.einsum('bkgqt,btkd->bqkgd', p, v)
    return o.reshape(B, Q, Q_W)


def _context_attention(q, k, v, sink):
    B, S = q.shape[:2]
    nb = S // BLOCK
    qb = q.reshape(B, nb, BLOCK, N_HEADS, HEAD_DIM).transpose(1, 0, 2, 3, 4)
    out = lax.map(lambda qi: _sink_attend(qi, k, v, sink), qb)
    return out.transpose(1, 0, 2, 3).reshape(B, S, Q_W)


def _latent_attention(q, k, v, k_ctx, v_ctx, sink):
    B, L = q.shape[:2]
    P = k_ctx.shape[1]
    nb = L // BLOCK
    pad = ((0, 0), (BLOCK, BLOCK), (0, 0), (0, 0))
    kp = jnp.pad(k, pad)
    vp = jnp.pad(v, pad)
    ctx_mask = jnp.ones((BLOCK, P), dtype=bool)

    def block(b):
        start = b * BLOCK
        qi = lax.dynamic_slice_in_dim(q, start, BLOCK, axis=1)
        kb = lax.dynamic_slice_in_dim(kp, start, 3 * BLOCK, axis=1)
        vb = lax.dynamic_slice_in_dim(vp, start, 3 * BLOCK, axis=1)
        qpos = start + jnp.arange(BLOCK)
        kpos = start - BLOCK + jnp.arange(3 * BLOCK)
        band = ((kpos[None, :] >= 0) & (kpos[None, :] < L)
                & (jnp.abs(qpos[:, None] - kpos[None, :]) <= WINDOW))
        keys = jnp.concatenate([kb, k_ctx], axis=1)
        vals = jnp.concatenate([vb, v_ctx], axis=1)
        mask = jnp.concatenate([band, ctx_mask], axis=1)
        return _sink_attend(qi, keys, vals, sink, mask)

    out = lax.map(block, jnp.arange(nb))
    return out.transpose(1, 0, 2, 3).reshape(B, L, Q_W)


def _layer(x, mod, attend, w_in, w_sconv, w_attn_o, w_conv_o, w_mix_out,
           g_pre_mix, g_post_mix, g_pre_ffn, w_ffn_up, w_ffn_conv, w_ffn_down, g_post_ffn):
    B, L = x.shape[:2]
    shift_a, scale_a, gate_a, shift_f, scale_f, gate_f = [mod[:, i][:, None, :] for i in range(N_MOD)]

    h = _rmsnorm(x, g_pre_mix) * (1 + scale_a) + shift_a
    proj = h @ w_in
    i0 = Q_W
    i1 = i0 + KV_W
    i2 = i1 + KV_W
    i3 = i2 + CONV_DIM
    i4 = i3 + CONV_DIM
    i5 = i4 + CONV_DIM
    i6 = i5 + D_MODEL
    q = proj[..., :i0].reshape(B, L, N_HEADS, HEAD_DIM)
    k = proj[..., i0:i1].reshape(B, L, N_KV_HEADS, HEAD_DIM)
    v = proj[..., i1:i2].reshape(B, L, N_KV_HEADS, HEAD_DIM)
    cb, cc, ch = proj[..., i2:i3], proj[..., i3:i4], proj[..., i4:i5]
    g_att, g_conv = proj[..., i5:i6], proj[..., i6:]

    a = attend(q, k, v)
    sc = cb * _dwconv3(cc * ch, w_sconv)
    merged = jax.nn.sigmoid(g_att) * (a @ w_attn_o) + jax.nn.sigmoid(g_conv) * (sc @ w_conv_o)
    x = x + gate_a * _rmsnorm(merged @ w_mix_out, g_post_mix)

    h = _rmsnorm(x, g_pre_ffn) * (1 + scale_f) + shift_f
    u = h @ w_ffn_up
    gt, val = u[..., :D_FF], u[..., D_FF:]
    f = (jax.nn.silu(_dwconv3(gt, w_ffn_conv)) * val) @ w_ffn_down
    x = x + gate_f * _rmsnorm(f, g_post_ffn)
    return x, k, v


def setup_inputs(seed: int = 0) -> dict:
    key = jax.random.key(seed)
    ks = jax.random.split(key, 24)
    f32 = jnp.float32

    def nrm(k, shape, s):
        return jax.random.normal(k, shape, f32) * s

    def gain(k):
        return 1.0 + 0.05 * jax.random.normal(k, (DEPTH, D_MODEL), f32)

    return {
        'x_prompt': nrm(ks[0], (BATCH, SEQ, D_MODEL), 1.0),
        'x_sample': nrm(ks[1], (DEC_BATCH, DEC_SEQ, D_MODEL), 1.0),
        'cache_k': nrm(ks[2], (DEC_BATCH, DEPTH, PAST_LEN, N_KV_HEADS, HEAD_DIM), 1.0),
        'cache_v': nrm(ks[3], (DEC_BATCH, DEPTH, PAST_LEN, N_KV_HEADS, HEAD_DIM), 1.0),
        'c': nrm(ks[4], (DEC_BATCH, D_MODEL), 1.0),
        'c_ctx': nrm(ks[5], (D_MODEL,), 1.0),
        'w_mod': nrm(ks[6], (DEPTH, D_MODEL, N_MOD * D_MODEL), 0.5 * D_MODEL ** -0.5),
        'b_mod': nrm(ks[7], (DEPTH, N_MOD * D_MODEL), 0.01),
        'g_pre_mix': gain(ks[8]),
        'w_in': nrm(ks[9], (DEPTH, D_MODEL, IN_W), D_MODEL ** -0.5),
        'w_sconv': nrm(ks[10], (DEPTH, CONV_K, CONV_DIM), CONV_K ** -0.5),
        'attn_sink': nrm(ks[11], (DEPTH, N_HEADS), 1.0),
        'w_attn_o': nrm(ks[12], (DEPTH, Q_W, D_MODEL), Q_W ** -0.5),
        'w_conv_o': nrm(ks[13], (DEPTH, CONV_DIM, D_MODEL), CONV_DIM ** -0.5),
        'w_mix_out': nrm(ks[14], (DEPTH, D_MODEL, D_MODEL), D_MODEL ** -0.5),
        'g_post_mix': gain(ks[15]),
        'g_pre_ffn': gain(ks[16]),
        'w_ffn_up': nrm(ks[17], (DEPTH, D_MODEL, 2 * D_FF), D_MODEL ** -0.5),
        'w_ffn_conv': nrm(ks[18], (DEPTH, CONV_K, D_FF), CONV_K ** -0.5),
        'w_ffn_down': nrm(ks[19], (DEPTH, D_FF, D_MODEL), D_FF ** -0.5),
        'g_post_ffn': gain(ks[20]),
    }


def reference(x_prompt, x_sample, cache_k, cache_v, c, c_ctx, w_mod, b_mod, g_pre_mix, w_in,
              w_sconv, attn_sink, w_attn_o, w_conv_o, w_mix_out, g_post_mix, g_pre_ffn,
              w_ffn_up, w_ffn_conv, w_ffn_down, g_post_ffn):
    cos, sin = _axial_rope_tables(x_sample.shape[1])
    xp, xs = x_prompt, x_sample
    new_k, new_v = [], []
    for l in range(DEPTH):
        lw = (w_in[l], w_sconv[l], w_attn_o[l], w_conv_o[l], w_mix_out[l], g_pre_mix[l],
              g_post_mix[l], g_pre_ffn[l], w_ffn_up[l], w_ffn_conv[l], w_ffn_down[l], g_post_ffn[l])
        sink = attn_sink[l]
        mod_ctx = _modulation(c_ctx[None, :], w_mod[l], b_mod[l])
        mod_lat = _modulation(c, w_mod[l], b_mod[l])

        def ctx_attend(q, k, v, sink=sink):
            return _context_attention(q, k, v, sink)

        def lat_attend(q, k, v, sink=sink, k_ctx=cache_k[:, l], v_ctx=cache_v[:, l]):
            return _latent_attention(_apply_axial_rope(q, cos, sin), _apply_axial_rope(k, cos, sin),
                                     v, k_ctx, v_ctx, sink)

        xp, k_l, v_l = _layer(xp, mod_ctx, ctx_attend, *lw)
        xs, _, _ = _layer(xs, mod_lat, lat_attend, *lw)
        new_k.append(k_l)
        new_v.append(v_l)
    k_state = jnp.stack(new_k, axis=1)
    v_state = jnp.stack(new_v, axis=1)
    return (xp, xs, k_state, v_state)
```

```python
import functools
import math

import jax
import jax.numpy as jnp
from jax import lax
from jax.experimental import pallas as pl
from jax.experimental.pallas import tpu as pltpu

F32 = jnp.float32
BF16 = jnp.bfloat16

HEAD_DIM = 128
N_HEADS = 32
N_KV_HEADS = 8
Q_PER_KV = N_HEADS // N_KV_HEADS
WINDOW_BLOCK = 128
GRID_W = 64
ROPE_THETA = 10000.0
EPS = 1e-6
N_MOD = 6
NEG = -1e30
SM_SCALE = HEAD_DIM ** -0.5

TM_PROJ = 2048
TM_MERGE = 1024
TM_DOWN = 1024
TN_WIDE = 512
TN_NARROW = 256
TM_ROWWISE = 256
TN_MOD = 512
MOD_ROWS = 16

VMEM_LIMIT_BYTES = 56 * 1024 * 1024


def _params(*semantics):
    return pltpu.CompilerParams(dimension_semantics=semantics,
                                vmem_limit_bytes=VMEM_LIMIT_BYTES)


def _dot(a, b):
    return jnp.dot(a, b, preferred_element_type=F32)


def _dot_nt(a, b):
    return lax.dot_general(a, b, (((1,), (1,)), ((), ())), preferred_element_type=F32)


def _resident(shape, index_map):
    return pl.BlockSpec(shape, index_map, pipeline_mode=pl.Buffered(1))


def _mod_kernel(cond_ref, w_ref, b_ref, o_ref):
    c = cond_ref[...]
    s = (c * jax.nn.sigmoid(c)).astype(BF16)
    o_ref[...] = _dot(s, w_ref[...].astype(BF16)) + b_ref[...]


def _modulation(cond, w_mod, b_mod):
    rows, d = cond.shape
    n = w_mod.shape[1]
    return pl.pallas_call(
        _mod_kernel,
        grid=(n // TN_MOD,),
        in_specs=[pl.BlockSpec((rows, d), lambda j: (0, 0)),
                  pl.BlockSpec((d, TN_MOD), lambda j: (0, j)),
                  pl.BlockSpec((1, TN_MOD), lambda j: (0, j))],
        out_specs=pl.BlockSpec((rows, TN_MOD), lambda j: (0, j)),
        out_shape=jax.ShapeDtypeStruct((rows, n), F32),
        compiler_params=_params("parallel"),
        name="modulation",
    )(cond, w_mod, b_mod.reshape(1, n))


def _rms(x, g):
    return x * lax.rsqrt(jnp.mean(x * x, axis=-1, keepdims=True) + EPS) * g


def _modulate(y, mod_ref, shift_idx, scale_idx):
    return y * (1 + mod_ref[0, scale_idx:scale_idx + 1, :]) + mod_ref[0, shift_idx:shift_idx + 1, :]


def _prenorm_kernel(x_ref, g_ref, mod_ref, o_ref):
    o_ref[...] = _modulate(_rms(x_ref[...], g_ref[...]), mod_ref, 0, 1).astype(o_ref.dtype)


def _post_mix_kernel(y_ref, x_ref, g_post_ref, g_pre_ref, mod_ref, x1_ref, h2_ref):
    x1 = x_ref[...] + mod_ref[0, 2:3, :] * _rms(y_ref[...], g_post_ref[...])
    x1_ref[...] = x1
    h2_ref[...] = _modulate(_rms(x1, g_pre_ref[...]), mod_ref, 3, 4).astype(h2_ref.dtype)


def _post_ffn_kernel(y_ref, x_ref, g_post_ref, mod_ref, o_ref):
    o_ref[...] = x_ref[...] + mod_ref[0, 5:6, :] * _rms(y_ref[...], g_post_ref[...])


def _rowwise_call(kernel, row_inputs, gains, mod, mod_map, out_dtypes, name):
    m, d = row_inputs[0].shape
    row_spec = pl.BlockSpec((TM_ROWWISE, d), lambda i: (i, 0))
    gain_spec = pl.BlockSpec((1, d), lambda i: (0, 0))
    mod_spec = pl.BlockSpec((1, N_MOD, d), lambda i: (mod_map(i, TM_ROWWISE), 0, 0))
    outs = pl.pallas_call(
        kernel,
        grid=(m // TM_ROWWISE,),
        in_specs=[row_spec] * len(row_inputs) + [gain_spec] * len(gains) + [mod_spec],
        out_specs=[row_spec] * len(out_dtypes),
        out_shape=[jax.ShapeDtypeStruct((m, d), dt) for dt in out_dtypes],
        compiler_params=_params("parallel"),
        name=name,
    )(*row_inputs, *gains, mod)
    return outs


def _linear_kernel(h_ref, w_ref, o_ref):
    o_ref[...] = _dot(h_ref[...], w_ref[...]).astype(o_ref.dtype)


def _rope_kernel(h_ref, w_ref, cos_ref, sin_ref, o_ref):
    y = _dot(h_ref[...], w_ref[...])
    cos = cos_ref[...]
    sin = sin_ref[...]
    lane = lax.broadcasted_iota(jnp.int32, cos.shape, 1)
    first_half = (lane % (HEAD_DIM // 2)) < (HEAD_DIM // 4)
    for hd in range(y.shape[1] // HEAD_DIM):
        x = y[:, hd * HEAD_DIM:(hd + 1) * HEAD_DIM]
        partner = jnp.where(first_half,
                            pltpu.roll(x, HEAD_DIM - HEAD_DIM // 4, axis=1),
                            pltpu.roll(x, HEAD_DIM // 4, axis=1))
        o_ref[:, hd * HEAD_DIM:(hd + 1) * HEAD_DIM] = (x * cos + partner * sin).astype(o_ref.dtype)


def _linear(h, w, col_block0, n_out, out_dtype, tm, tn, name, rope=None):
    m, k = h.shape
    in_specs = [_resident((tm, k), lambda i, j: (i, 0)),
                pl.BlockSpec((k, tn), lambda i, j: (0, col_block0 + j))]
    args = [h, w]
    kernel = _linear_kernel
    if rope is not None:
        assert tm == rope[0].shape[0]
        in_specs += [pl.BlockSpec((tm, HEAD_DIM), lambda i, j: (0, 0))] * 2
        args += list(rope)
        kernel = _rope_kernel
    return pl.pallas_call(
        kernel,
        grid=(m // tm, n_out // tn),
        in_specs=in_specs,
        out_specs=pl.BlockSpec((tm, tn), lambda i, j: (i, j)),
        out_shape=jax.ShapeDtypeStruct((m, n_out), out_dtype),
        compiler_params=_params("parallel", "arbitrary"),
        name=name,
    )(*args)


def _dwconv3_rows(x, w, seq_len):
    rows = x.shape[0]
    assert seq_len & (seq_len - 1) == 0 and rows % seq_len == 0
    t = lax.broadcasted_iota(jnp.int32, (rows, 1), 0) & (seq_len - 1)
    prev = jnp.where(t == 0, 0.0, pltpu.roll(x, 1, axis=0))
    nxt = jnp.where(t == seq_len - 1, 0.0, pltpu.roll(x, rows - 1, axis=0))
    return prev * w[0:1, :] + x * w[1:2, :] + nxt * w[2:3, :]


def _sconv_kernel(h_ref, wb_ref, wc_ref, wh_ref, cw_ref, o_ref, *, seq_len):
    h = h_ref[...]
    conv = _dwconv3_rows(_dot(h, wc_ref[...]) * _dot(h, wh_ref[...]), cw_ref[...], seq_len)
    o_ref[...] = (_dot(h, wb_ref[...]) * conv).astype(o_ref.dtype)


def _short_conv(h, w_in, col_blocks, conv_w, seq_len, name):
    m, k = h.shape
    width = conv_w.shape[1]
    tm, tn = TM_PROJ, TN_NARROW
    w_specs = [pl.BlockSpec((k, tn), functools.partial(lambda i, j, c0: (0, c0 + j), c0=c0))
               for c0 in col_blocks]
    return pl.pallas_call(
        functools.partial(_sconv_kernel, seq_len=seq_len),
        grid=(m // tm, width // tn),
        in_specs=[_resident((tm, k), lambda i, j: (i, 0))] + w_specs
                 + [pl.BlockSpec((conv_w.shape[0], tn), lambda i, j: (0, j))],
        out_specs=pl.BlockSpec((tm, tn), lambda i, j: (i, j)),
        out_shape=jax.ShapeDtypeStruct((m, width), BF16),
        compiler_params=_params("parallel", "arbitrary"),
        name=name,
    )(h, w_in, w_in, w_in, conv_w)


def _merge_kernel(h_ref, a_ref, sc_ref, wga_ref, wgc_ref, wo_ref, wc_ref, o_ref):
    h = h_ref[...]
    att = jax.nn.sigmoid(_dot(h, wga_ref[...])) * _dot(a_ref[...], wo_ref[...])
    conv = jax.nn.sigmoid(_dot(h, wgc_ref[...])) * _dot(sc_ref[...], wc_ref[...])
    o_ref[...] = (att + conv).astype(o_ref.dtype)


def _merge(h, a, sc, w_in, gate_col_blocks, w_attn_o, w_conv_o, name):
    m, d = h.shape
    tm, tn = TM_MERGE, TN_NARROW
    ga0, gc0 = gate_col_blocks
    return pl.pallas_call(
        _merge_kernel,
        grid=(m // tm, d // tn),
        in_specs=[_resident((tm, d), lambda i, j: (i, 0)),
                  _resident((tm, a.shape[1]), lambda i, j: (i, 0)),
                  _resident((tm, sc.shape[1]), lambda i, j: (i, 0)),
                  pl.BlockSpec((d, tn), lambda i, j: (0, ga0 + j)),
                  pl.BlockSpec((d, tn), lambda i, j: (0, gc0 + j)),
                  pl.BlockSpec((a.shape[1], tn), lambda i, j: (0, j)),
                  pl.BlockSpec((sc.shape[1], tn), lambda i, j: (0, j))],
        out_specs=pl.BlockSpec((tm, tn), lambda i, j: (i, j)),
        out_shape=jax.ShapeDtypeStruct((m, d), BF16),
        compiler_params=_params("parallel", "arbitrary"),
        name=name,
    )(h, a, sc, w_in, w_in, w_attn_o, w_conv_o)


def _ffn_up_kernel(h_ref, wg_ref, wv_ref, cw_ref, o_ref, *, seq_len):
    h = h_ref[...]
    gate = _dwconv3_rows(_dot(h, wg_ref[...]), cw_ref[...], seq_len)
    o_ref[...] = (gate * jax.nn.sigmoid(gate) * _dot(h, wv_ref[...])).astype(o_ref.dtype)


def _ffn_up(h, w_up, conv_w, seq_len, name):
    m, k = h.shape
    d_ff = conv_w.shape[1]
    tm, tn = TM_PROJ, TN_NARROW
    val0 = d_ff // tn
    return pl.pallas_call(
        functools.partial(_ffn_up_kernel, seq_len=seq_len),
        grid=(m // tm, d_ff // tn),
        in_specs=[_resident((tm, k), lambda i, j: (i, 0)),
                  pl.BlockSpec((k, tn), lambda i, j: (0, j)),
                  pl.BlockSpec((k, tn), lambda i, j: (0, val0 + j)),
                  pl.BlockSpec((conv_w.shape[0], tn), lambda i, j: (0, j))],
        out_specs=pl.BlockSpec((tm, tn), lambda i, j: (i, j)),
        out_shape=jax.ShapeDtypeStruct((m, d_ff), BF16),
        compiler_params=_params("parallel", "arbitrary"),
        name=name,
    )(h, w_up, w_up, conv_w)


def _softmax_rows(scores, sink):
    m = sink
    for s in scores:
        m = jnp.maximum(m, jnp.max(s, axis=-1, keepdims=True))
    ps = [jnp.exp(s - m) for s in scores]
    denom = jnp.exp(sink - m)
    for p in ps:
        denom = denom + jnp.sum(p, axis=-1, keepdims=True)
    return [(p / denom).astype(BF16) for p in ps]


def _group_queries(q_ref, sink_ref, kv, rows):
    heads = [kv * Q_PER_KV + g for g in range(Q_PER_KV)]
    q = jnp.concatenate([q_ref[:, h * HEAD_DIM:(h + 1) * HEAD_DIM] for h in heads], axis=0)
    sink = jnp.concatenate([jnp.full((rows, 1), sink_ref[h], F32) for h in heads], axis=0)
    return heads, q, sink


def _ctx_attn_kernel(sink_ref, q_ref, k_ref, v_ref, o_ref):
    rows = q_ref.shape[0]
    for kv in range(N_KV_HEADS):
        cols = slice(kv * HEAD_DIM, (kv + 1) * HEAD_DIM)
        heads, q, sink = _group_queries(q_ref, sink_ref, kv, rows)
        s = _dot_nt(q, k_ref[:, cols].astype(BF16)) * SM_SCALE
        (p,) = _softmax_rows([s], sink)
        o = _dot(p, v_ref[:, cols].astype(BF16))
        for g, h in enumerate(heads):
            o_ref[:, h * HEAD_DIM:(h + 1) * HEAD_DIM] = o[g * rows:(g + 1) * rows].astype(o_ref.dtype)


def _context_attention(q, k, v, sink, seq_len, name):
    m = q.shape[0]
    return pl.pallas_call(
        _ctx_attn_kernel,
        grid=(m // seq_len,),
        in_specs=[pl.BlockSpec(memory_space=pltpu.SMEM),
                  pl.BlockSpec((seq_len, q.shape[1]), lambda b: (b, 0)),
                  pl.BlockSpec((seq_len, k.shape[1]), lambda b: (b, 0)),
                  pl.BlockSpec((seq_len, v.shape[1]), lambda b: (b, 0))],
        out_specs=pl.BlockSpec((seq_len, q.shape[1]), lambda b: (b, 0)),
        out_shape=jax.ShapeDtypeStruct(q.shape, BF16),
        compiler_params=_params("parallel"),
        name=name,
    )(sink, q, k, v)


def _lat_attn_kernel(sink_ref, q_ref, kp_ref, kc_ref, kn_ref, vp_ref, vc_ref, vn_ref,
                     kx_ref, vx_ref, o_ref, kx_bf, vx_bf):
    blk = WINDOW_BLOCK
    qb = pl.program_id(1)
    n_blocks = pl.num_programs(1)

    @pl.when(qb == 0)
    def _():
        kx_bf[...] = kx_ref[0].astype(BF16)
        vx_bf[...] = vx_ref[0].astype(BF16)

    shape = (Q_PER_KV * blk, 3 * blk)
    r = lax.broadcasted_iota(jnp.int32, shape, 0) & (blk - 1)
    c = lax.broadcasted_iota(jnp.int32, shape, 1)
    in_window = (c >= r) & (c <= r + 2 * blk)
    in_sequence = ((c >= blk) | (qb > 0)) & ((c < 2 * blk) | (qb < n_blocks - 1))
    band_mask = in_window & in_sequence

    for kv in range(N_KV_HEADS):
        cols = slice(kv * HEAD_DIM, (kv + 1) * HEAD_DIM)
        heads, q, sink = _group_queries(q_ref, sink_ref, kv, blk)
        k_band = jnp.concatenate([kp_ref[:, cols], kc_ref[:, cols], kn_ref[:, cols]], axis=0)
        v_band = jnp.concatenate([vp_ref[:, cols], vc_ref[:, cols], vn_ref[:, cols]], axis=0)
        s_band = jnp.where(band_mask, _dot_nt(q, k_band) * SM_SCALE, NEG)
        s_ctx = _dot_nt(q, kx_bf[:, cols]) * SM_SCALE
        p_band, p_ctx = _softmax_rows([s_band, s_ctx], sink)
        o = _dot(p_band, v_band) + _dot(p_ctx, vx_bf[:, cols])
        for g, h in enumerate(heads):
            o_ref[:, h * HEAD_DIM:(h + 1) * HEAD_DIM] = o[g * blk:(g + 1) * blk].astype(o_ref.dtype)


def _latent_attention(q, k, v, k_ctx, v_ctx, sink, seq_len, name):
    m, q_w = q.shape
    kv_w = k.shape[1]
    n_batch, past, _ = k_ctx.shape
    blk = WINDOW_BLOCK
    nb = seq_len // blk

    def band_spec(offset):
        def index_map(b, i):
            return (b * nb + jnp.clip(i + offset, 0, nb - 1), 0)
        return pl.BlockSpec((blk, kv_w), index_map)

    ctx_spec = pl.BlockSpec((1, past, kv_w), lambda b, i: (b, 0, 0))
    return pl.pallas_call(
        _lat_attn_kernel,
        grid=(n_batch, nb),
        in_specs=[pl.BlockSpec(memory_space=pltpu.SMEM),
                  pl.BlockSpec((blk, q_w), lambda b, i: (b * nb + i, 0)),
                  band_spec(-1), band_spec(0), band_spec(1),
                  band_spec(-1), band_spec(0), band_spec(1),
                  ctx_spec, ctx_spec],
        out_specs=pl.BlockSpec((blk, q_w), lambda b, i: (b * nb + i, 0)),
        out_shape=jax.ShapeDtypeStruct((m, q_w), BF16),
        scratch_shapes=[pltpu.VMEM((past, kv_w), BF16), pltpu.VMEM((past, kv_w), BF16)],
        compiler_params=_params("parallel", "arbitrary"),
        name=name,
    )(sink, q, k, k, k, v, v, v, k_ctx, v_ctx)


def _rope_tables(seq_len):
    rows = seq_len // GRID_W
    row = jnp.repeat(jnp.arange(rows), GRID_W).astype(F32)
    col = jnp.tile(jnp.arange(GRID_W), rows).astype(F32)
    n_freq = HEAD_DIM // 4
    inv = ROPE_THETA ** (-jnp.arange(n_freq, dtype=F32) / n_freq)
    cos_r, sin_r = jnp.cos(row[:, None] * inv), jnp.sin(row[:, None] * inv)
    cos_c, sin_c = jnp.cos(col[:, None] * inv), jnp.sin(col[:, None] * inv)
    cos = jnp.concatenate([cos_r, cos_r, cos_c, cos_c], axis=-1)
    sin = jnp.concatenate([-sin_r, sin_r, -sin_c, sin_c], axis=-1)
    return cos, sin


def _layer(x, mod, mod_map, seq_len, weights, sink, tag, latent_ctx=None):
    (w_in, w_sconv, w_attn_o, w_conv_o, w_mix_out, g_pre_mix, g_post_mix, g_pre_ffn,
     w_ffn_up, w_ffn_conv, w_ffn_down, g_post_ffn) = weights
    m, d = x.shape
    q_w = N_HEADS * HEAD_DIM
    kv_w = N_KV_HEADS * HEAD_DIM
    conv_dim = w_sconv.shape[1]
    latent = latent_ctx is not None
    rope = _rope_tables(seq_len) if latent else None
    kv_dtype = BF16 if latent else F32

    (h,) = _rowwise_call(_prenorm_kernel, [x], [g_pre_mix], mod, mod_map, [BF16], f"prenorm_{tag}")

    k0 = q_w
    v0 = k0 + kv_w
    b0 = v0 + kv_w
    c0 = b0 + conv_dim
    h0 = c0 + conv_dim
    ga0 = h0 + conv_dim
    gc0 = ga0 + d
    q = _linear(h, w_in, 0, q_w, BF16, TM_PROJ, TN_WIDE, f"q_proj_{tag}", rope)
    k = _linear(h, w_in, k0 // TN_WIDE, kv_w, kv_dtype, TM_PROJ, TN_WIDE, f"k_proj_{tag}", rope)
    v = _linear(h, w_in, v0 // TN_WIDE, kv_w, kv_dtype, TM_PROJ, TN_WIDE, f"v_proj_{tag}")

    if latent:
        a = _latent_attention(q, k, v, latent_ctx[0], latent_ctx[1], sink, seq_len, f"attn_{tag}")
    else:
        a = _context_attention(q, k, v, sink, seq_len, f"attn_{tag}")

    sc = _short_conv(h, w_in, [b0 // TN_NARROW, c0 // TN_NARROW, h0 // TN_NARROW],
                     w_sconv, seq_len, f"short_conv_{tag}")
    merged = _merge(h, a, sc, w_in, (ga0 // TN_NARROW, gc0 // TN_NARROW), w_attn_o, w_conv_o,
                    f"merge_{tag}")
    y = _linear(merged, w_mix_out, 0, d, F32, TM_PROJ, TN_WIDE, f"mix_out_{tag}")
    x1, h2 = _rowwise_call(_post_mix_kernel, [y, x], [g_post_mix, g_pre_ffn], mod, mod_map,
                           [F32, BF16], f"post_mix_{tag}")

    f = _ffn_up(h2, w_ffn_up, w_ffn_conv, seq_len, f"ffn_up_{tag}")
    y2 = _linear(f, w_ffn_down, 0, d, F32, TM_DOWN, TN_NARROW, f"ffn_down_{tag}")
    (out,) = _rowwise_call(_post_ffn_kernel, [y2, x1], [g_post_ffn], mod, mod_map, [F32],
                           f"post_ffn_{tag}")
    return out, k, v


def kernel(x_prompt, x_sample, cache_k, cache_v, c, c_ctx, w_mod, b_mod, g_pre_mix, w_in, w_sconv, attn_sink, w_attn_o, w_conv_o, w_mix_out, g_post_mix, g_pre_ffn, w_ffn_up, w_ffn_conv, w_ffn_down, g_post_ffn):
    batch, seq, d = x_prompt.shape
    dec_batch, dec_seq, _ = x_sample.shape
    depth = w_in.shape[0]
    past = cache_k.shape[2]
    kv_w = N_KV_HEADS * HEAD_DIM
    assert 1 + dec_batch <= MOD_ROWS

    cond = jnp.concatenate([c_ctx[None, :], c, jnp.zeros((MOD_ROWS - 1 - dec_batch, d), F32)], axis=0)

    def ctx_mod_map(i, tm):
        return 0

    def lat_mod_map(i, tm):
        return 1 + i // (dec_seq // tm)

    xp = x_prompt.reshape(batch * seq, d)
    xs = x_sample.reshape(dec_batch * dec_seq, d)
    new_k, new_v = [], []
    for l in range(depth):
        weights = (w_in[l].astype(BF16), w_sconv[l], w_attn_o[l].astype(BF16),
                   w_conv_o[l].astype(BF16), w_mix_out[l].astype(BF16),
                   g_pre_mix[l][None, :], g_post_mix[l][None, :], g_pre_ffn[l][None, :],
                   w_ffn_up[l].astype(BF16), w_ffn_conv[l], w_ffn_down[l].astype(BF16),
                   g_post_ffn[l][None, :])
        mod = _modulation(cond, w_mod[l], b_mod[l]).reshape(MOD_ROWS, N_MOD, d)
        k_ctx = cache_k[:, l].reshape(dec_batch, past, kv_w)
        v_ctx = cache_v[:, l].reshape(dec_batch, past, kv_w)
        xp, k_l, v_l = _layer(xp, mod, ctx_mod_map, seq, weights, attn_sink[l], f"ctx{l}")
        xs, _, _ = _layer(xs, mod, lat_mod_map, dec_seq, weights, attn_sink[l], f"lat{l}",
                          latent_ctx=(k_ctx, v_ctx))
        new_k.append(k_l.reshape(batch, seq, N_KV_HEADS, HEAD_DIM))
        new_v.append(v_l.reshape(batch, seq, N_KV_HEADS, HEAD_DIM))
    return (xp.reshape(batch, seq, d), xs.reshape(dec_batch, dec_seq, d),
            jnp.stack(new_k, axis=1), jnp.stack(new_v, axis=1))
```

```python
import functools
import math

import jax
import jax.numpy as jnp
from jax import lax
from jax.experimental import pallas as pl
from jax.experimental.pallas import tpu as pltpu

F32 = jnp.float32
BF16 = jnp.bfloat16

HEAD_DIM = 128
N_HEADS = 32
N_KV_HEADS = 8
Q_PER_KV = N_HEADS // N_KV_HEADS
WINDOW_BLOCK = 128
GRID_W = 64
ROPE_THETA = 10000.0
EPS = 1e-6
N_MOD = 6
NEG = -1e30
SM_SCALE = HEAD_DIM ** -0.5
LOG2_E = math.log2(math.e)

TM_PROJ = 2048
TM_MERGE = 1024
TM_QKV = 1024
TM_POST = 512
EPILOGUE_ROWS = 64
TN_WIDE = 512
TN_NARROW = 256
TM_ROWWISE = 256
TN_MOD = 512
MOD_ROWS = 16

VMEM_LIMIT_BYTES = 56 * 1024 * 1024


def _params(*semantics):
    return pltpu.CompilerParams(dimension_semantics=semantics,
                                vmem_limit_bytes=VMEM_LIMIT_BYTES)


def _dot(a, b):
    return jnp.dot(a, b, preferred_element_type=F32)


def _dot_nt(a, b):
    return lax.dot_general(a, b, (((1,), (1,)), ((), ())), preferred_element_type=F32)


def _resident(shape, index_map):
    return pl.BlockSpec(shape, index_map, pipeline_mode=pl.Buffered(1))


def _mod_kernel(cond_ref, w_ref, b_ref, o_ref):
    c = cond_ref[...]
    s = (c * jax.nn.sigmoid(c)).astype(BF16)
    o_ref[...] = _dot(s, w_ref[...].astype(BF16)) + b_ref[...]


def _modulation(cond, w_mod, b_mod):
    rows, d = cond.shape
    n = w_mod.shape[1]
    return pl.pallas_call(
        _mod_kernel,
        grid=(n // TN_MOD,),
        in_specs=[pl.BlockSpec((rows, d), lambda j: (0, 0)),
                  pl.BlockSpec((d, TN_MOD), lambda j: (0, j)),
                  pl.BlockSpec((1, TN_MOD), lambda j: (0, j))],
        out_specs=pl.BlockSpec((rows, TN_MOD), lambda j: (0, j)),
        out_shape=jax.ShapeDtypeStruct((rows, n), F32),
        compiler_params=_params("parallel"),
        name="modulation",
    )(cond, w_mod, b_mod.reshape(1, n))


def _rms(x, g):
    return x * lax.rsqrt(jnp.mean(x * x, axis=-1, keepdims=True) + EPS) * g


def _modulate(y, mod_ref, shift_idx, scale_idx):
    return y * (1 + mod_ref[0, scale_idx:scale_idx + 1, :]) + mod_ref[0, shift_idx:shift_idx + 1, :]


def _prenorm_kernel(x_ref, g_ref, mod_ref, o_ref):
    o_ref[...] = _modulate(_rms(x_ref[...], g_ref[...]), mod_ref, 0, 1).astype(o_ref.dtype)


def _prenorm(x, gain, mod, mod_map, name):
    m, d = x.shape
    row_spec = pl.BlockSpec((TM_ROWWISE, d), lambda i: (i, 0))
    return pl.pallas_call(
        _prenorm_kernel,
        grid=(m // TM_ROWWISE,),
        in_specs=[row_spec, pl.BlockSpec((1, d), lambda i: (0, 0)),
                  pl.BlockSpec((1, N_MOD, d), lambda i: (mod_map(i, TM_ROWWISE), 0, 0))],
        out_specs=row_spec,
        out_shape=jax.ShapeDtypeStruct((m, d), BF16),
        compiler_params=_params("parallel"),
        name=name,
    )(x, gain, mod)


def _store_column_tile(acc_ref, y):
    tn = y.shape[1]
    col = pl.multiple_of(pl.program_id(1) * tn, tn)
    acc_ref[:, pl.ds(col, tn)] = y


def _is_last_column_tile():
    return pl.program_id(1) == pl.num_programs(1) - 1


def _for_row_chunks(n_rows, body):
    def step(c, carry):
        body(pl.ds(pl.multiple_of(c * EPILOGUE_ROWS, EPILOGUE_ROWS), EPILOGUE_ROWS))
        return carry
    lax.fori_loop(0, n_rows // EPILOGUE_ROWS, step, 0)


def _mix_post_kernel(m_ref, w_ref, x_ref, g_post_ref, g_pre_ref, mod_ref, x1_ref, h2_ref):
    _store_column_tile(x1_ref, _dot(m_ref[...], w_ref[...]))

    @pl.when(_is_last_column_tile())
    def _():
        def rows_body(rows):
            x1 = x_ref[rows, :] + mod_ref[0, 2:3, :] * _rms(x1_ref[rows, :], g_post_ref[...])
            x1_ref[rows, :] = x1
            h2_ref[rows, :] = _modulate(_rms(x1, g_pre_ref[...]), mod_ref, 3, 4).astype(h2_ref.dtype)
        _for_row_chunks(x1_ref.shape[0], rows_body)


def _down_post_kernel(f_ref, w_ref, x_ref, g_post_ref, mod_ref, o_ref):
    _store_column_tile(o_ref, _dot(f_ref[...], w_ref[...]))

    @pl.when(_is_last_column_tile())
    def _():
        def rows_body(rows):
            o_ref[rows, :] = x_ref[rows, :] + mod_ref[0, 5:6, :] * _rms(o_ref[rows, :], g_post_ref[...])
        _for_row_chunks(o_ref.shape[0], rows_body)


def _linear_post(kernel, lhs, w, x, gains, mod, mod_map, out_dtypes, tn, lhs_spec, name):
    m, k = lhs.shape
    d = w.shape[1]
    tm = TM_POST
    row_block = lambda i, j: (i, 0)
    return pl.pallas_call(
        kernel,
        grid=(m // tm, d // tn),
        in_specs=[lhs_spec((tm, k), row_block),
                  pl.BlockSpec((k, tn), lambda i, j: (0, j)),
                  _resident((tm, d), row_block)]
                 + [pl.BlockSpec((1, d), lambda i, j: (0, 0))] * len(gains)
                 + [pl.BlockSpec((1, N_MOD, d), lambda i, j: (mod_map(i, tm), 0, 0))],
        out_specs=[pl.BlockSpec((tm, d), row_block)] * len(out_dtypes),
        out_shape=[jax.ShapeDtypeStruct((m, d), dt) for dt in out_dtypes],
        compiler_params=_params("parallel", "arbitrary"),
        name=name,
    )(lhs, w, x, *gains, mod)


def _linear_kernel(h_ref, w_ref, o_ref):
    o_ref[...] = _dot(h_ref[...], w_ref[...]).astype(o_ref.dtype)


def _rope_kernel(h_ref, w_ref, cos_ref, sin_ref, o_ref):
    y = _dot(h_ref[...], w_ref[...])
    cos = cos_ref[...]
    sin = sin_ref[...]
    lane = lax.broadcasted_iota(jnp.int32, cos.shape, 1)
    first_half = (lane % (HEAD_DIM // 2)) < (HEAD_DIM // 4)
    for hd in range(y.shape[1] // HEAD_DIM):
        x = y[:, hd * HEAD_DIM:(hd + 1) * HEAD_DIM]
        partner = jnp.where(first_half,
                            pltpu.roll(x, HEAD_DIM - HEAD_DIM // 4, axis=1),
                            pltpu.roll(x, HEAD_DIM // 4, axis=1))
        o_ref[:, hd * HEAD_DIM:(hd + 1) * HEAD_DIM] = (x * cos + partner * sin).astype(o_ref.dtype)


def _linear(h, w, col_block0, n_out, out_dtype, tm, tn, name, rope=None):
    m, k = h.shape
    in_specs = [pl.BlockSpec((tm, k), lambda i, j: (i, 0)),
                pl.BlockSpec((k, tn), lambda i, j: (0, col_block0 + j))]
    args = [h, w]
    kernel = _linear_kernel
    if rope is not None:
        tiles_per_seq = rope[0].shape[0] // tm
        in_specs += [pl.BlockSpec((tm, HEAD_DIM), lambda i, j: (i % tiles_per_seq, 0))] * 2
        args += list(rope)
        kernel = _rope_kernel
    return pl.pallas_call(
        kernel,
        grid=(m // tm, n_out // tn),
        in_specs=in_specs,
        out_specs=pl.BlockSpec((tm, tn), lambda i, j: (i, j)),
        out_shape=jax.ShapeDtypeStruct((m, n_out), out_dtype),
        compiler_params=_params("parallel", "arbitrary"),
        name=name,
    )(*args)


def _dwconv3_rows(x, w, seq_len):
    rows = x.shape[0]
    assert seq_len & (seq_len - 1) == 0 and rows % seq_len == 0
    t = lax.broadcasted_iota(jnp.int32, (rows, 1), 0) & (seq_len - 1)
    prev = jnp.where(t == 0, 0.0, pltpu.roll(x, 1, axis=0))
    nxt = jnp.where(t == seq_len - 1, 0.0, pltpu.roll(x, rows - 1, axis=0))
    return prev * w[0:1, :] + x * w[1:2, :] + nxt * w[2:3, :]


def _sconv_kernel(h_ref, wb_ref, wc_ref, wh_ref, cw_ref, o_ref, *, seq_len):
    h = h_ref[...]
    conv = _dwconv3_rows(_dot(h, wc_ref[...]) * _dot(h, wh_ref[...]), cw_ref[...], seq_len)
    o_ref[...] = (_dot(h, wb_ref[...]) * conv).astype(o_ref.dtype)


def _short_conv(h, w_in, col_blocks, conv_w, seq_len, name):
    m, k = h.shape
    width = conv_w.shape[1]
    tm, tn = TM_PROJ, TN_NARROW
    w_specs = [pl.BlockSpec((k, tn), functools.partial(lambda i, j, c0: (0, c0 + j), c0=c0))
               for c0 in col_blocks]
    return pl.pallas_call(
        functools.partial(_sconv_kernel, seq_len=seq_len),
        grid=(m // tm, width // tn),
        in_specs=[_resident((tm, k), lambda i, j: (i, 0))] + w_specs
                 + [pl.BlockSpec((conv_w.shape[0], tn), lambda i, j: (0, j))],
        out_specs=pl.BlockSpec((tm, tn), lambda i, j: (i, j)),
        out_shape=jax.ShapeDtypeStruct((m, width), BF16),
        compiler_params=_params("parallel", "arbitrary"),
        name=name,
    )(h, w_in, w_in, w_in, conv_w)


def _merge_kernel(h_ref, a_ref, sc_ref, wga_ref, wgc_ref, wo_ref, wc_ref, o_ref):
    h = h_ref[...]
    att = jax.nn.sigmoid(_dot(h, wga_ref[...])) * _dot(a_ref[...], wo_ref[...])
    conv = jax.nn.sigmoid(_dot(h, wgc_ref[...])) * _dot(sc_ref[...], wc_ref[...])
    o_ref[...] = (att + conv).astype(o_ref.dtype)


def _merge(h, a, sc, w_in, gate_col_blocks, w_attn_o, w_conv_o, name):
    m, d = h.shape
    tm, tn = TM_MERGE, TN_NARROW
    ga0, gc0 = gate_col_blocks
    return pl.pallas_call(
        _merge_kernel,
        grid=(m // tm, d // tn),
        in_specs=[_resident((tm, d), lambda i, j: (i, 0)),
                  _resident((tm, a.shape[1]), lambda i, j: (i, 0)),
                  _resident((tm, sc.shape[1]), lambda i, j: (i, 0)),
                  pl.BlockSpec((d, tn), lambda i, j: (0, ga0 + j)),
                  pl.BlockSpec((d, tn), lambda i, j: (0, gc0 + j)),
                  pl.BlockSpec((a.shape[1], tn), lambda i, j: (0, j)),
                  pl.BlockSpec((sc.shape[1], tn), lambda i, j: (0, j))],
        out_specs=pl.BlockSpec((tm, tn), lambda i, j: (i, j)),
        out_shape=jax.ShapeDtypeStruct((m, d), BF16),
        compiler_params=_params("parallel", "arbitrary"),
        name=name,
    )(h, a, sc, w_in, w_in, w_attn_o, w_conv_o)


def _ffn_up_kernel(h_ref, wg_ref, wv_ref, cw_ref, o_ref, *, seq_len):
    h = h_ref[...]
    gate = _dwconv3_rows(_dot(h, wg_ref[...]), cw_ref[...], seq_len)
    o_ref[...] = (gate * jax.nn.sigmoid(gate) * _dot(h, wv_ref[...])).astype(o_ref.dtype)


def _ffn_up(h, w_up, conv_w, seq_len, name):
    m, k = h.shape
    d_ff = conv_w.shape[1]
    tm, tn = TM_PROJ, TN_NARROW
    val0 = d_ff // tn
    return pl.pallas_call(
        functools.partial(_ffn_up_kernel, seq_len=seq_len),
        grid=(m // tm, d_ff // tn),
        in_specs=[_resident((tm, k), lambda i, j: (i, 0)),
                  pl.BlockSpec((k, tn), lambda i, j: (0, j)),
                  pl.BlockSpec((k, tn), lambda i, j: (0, val0 + j)),
                  pl.BlockSpec((conv_w.shape[0], tn), lambda i, j: (0, j))],
        out_specs=pl.BlockSpec((tm, tn), lambda i, j: (i, j)),
        out_shape=jax.ShapeDtypeStruct((m, d_ff), BF16),
        compiler_params=_params("parallel", "arbitrary"),
        name=name,
    )(h, w_up, w_up, conv_w)


def _with_ones_column(v):
    lane = lax.broadcasted_iota(jnp.int32, v.shape, 1)
    return jnp.concatenate([v, (lane == 0).astype(v.dtype)], axis=1)


def _sink_attend(raw_scores, values, sink):
    raw_max = raw_scores[0].max(axis=-1, keepdims=True)
    for s in raw_scores[1:]:
        raw_max = jnp.maximum(raw_max, s.max(axis=-1, keepdims=True))
    m = jnp.maximum(raw_max * SM_SCALE, sink)
    m_log2 = m * LOG2_E
    acc = None
    for s, v in zip(raw_scores, values):
        p = jnp.exp2(s * (SM_SCALE * LOG2_E) - m_log2)
        pv = _dot(p.astype(BF16), _with_ones_column(v))
        acc = pv if acc is None else acc + pv
    denom = acc[:, HEAD_DIM:HEAD_DIM + 1] + jnp.exp(sink - m)
    return acc[:, :HEAD_DIM] / denom


def _group_queries(q_ref, sink_ref, kv, rows):
    heads = [kv * Q_PER_KV + g for g in range(Q_PER_KV)]
    q = jnp.concatenate([q_ref[:, h * HEAD_DIM:(h + 1) * HEAD_DIM] for h in heads], axis=0)
    sink = jnp.concatenate([jnp.full((rows, 1), sink_ref[h], F32) for h in heads], axis=0)
    return heads, q, sink


def _ctx_attn_kernel(sink_ref, q_ref, k_ref, v_ref, o_ref):
    rows = q_ref.shape[0]
    for kv in range(N_KV_HEADS):
        cols = slice(kv * HEAD_DIM, (kv + 1) * HEAD_DIM)
        heads, q, sink = _group_queries(q_ref, sink_ref, kv, rows)
        s = _dot_nt(q, k_ref[:, cols].astype(BF16))
        o = _sink_attend([s], [v_ref[:, cols].astype(BF16)], sink)
        for g, h in enumerate(heads):
            o_ref[:, h * HEAD_DIM:(h + 1) * HEAD_DIM] = o[g * rows:(g + 1) * rows].astype(o_ref.dtype)


def _context_attention(q, k, v, sink, seq_len, name):
    m = q.shape[0]
    return pl.pallas_call(
        _ctx_attn_kernel,
        grid=(m // seq_len,),
        in_specs=[pl.BlockSpec(memory_space=pltpu.SMEM),
                  pl.BlockSpec((seq_len, q.shape[1]), lambda b: (b, 0)),
                  pl.BlockSpec((seq_len, k.shape[1]), lambda b: (b, 0)),
                  pl.BlockSpec((seq_len, v.shape[1]), lambda b: (b, 0))],
        out_specs=pl.BlockSpec((seq_len, q.shape[1]), lambda b: (b, 0)),
        out_shape=jax.ShapeDtypeStruct(q.shape, BF16),
        compiler_params=_params("parallel"),
        name=name,
    )(sink, q, k, v)


def _lat_attn_kernel(sink_ref, q_ref, kp_ref, kc_ref, kn_ref, vp_ref, vc_ref, vn_ref,
                     kx_ref, vx_ref, o_ref, kx_bf, vx_bf):
    blk = WINDOW_BLOCK
    qb = pl.program_id(1)
    n_blocks = pl.num_programs(1)

    @pl.when(qb == 0)
    def _():
        kx_bf[...] = kx_ref[0].astype(BF16)
        vx_bf[...] = vx_ref[0].astype(BF16)

    shape = (Q_PER_KV * blk, 3 * blk)
    r = lax.broadcasted_iota(jnp.int32, shape, 0) & (blk - 1)
    c = lax.broadcasted_iota(jnp.int32, shape, 1)
    in_window = (c >= r) & (c <= r + 2 * blk)
    in_sequence = ((c >= blk) | (qb > 0)) & ((c < 2 * blk) | (qb < n_blocks - 1))
    band_mask = in_window & in_sequence

    for kv in range(N_KV_HEADS):
        cols = slice(kv * HEAD_DIM, (kv + 1) * HEAD_DIM)
        heads, q, sink = _group_queries(q_ref, sink_ref, kv, blk)
        k_band = jnp.concatenate([kp_ref[:, cols], kc_ref[:, cols], kn_ref[:, cols]], axis=0)
        v_band = jnp.concatenate([vp_ref[:, cols], vc_ref[:, cols], vn_ref[:, cols]], axis=0)
        s_band = jnp.where(band_mask, _dot_nt(q, k_band), NEG)
        s_ctx = _dot_nt(q, kx_bf[:, cols])
        o = _sink_attend([s_band, s_ctx], [v_band, vx_bf[:, cols]], sink)
        for g, h in enumerate(heads):
            o_ref[:, h * HEAD_DIM:(h + 1) * HEAD_DIM] = o[g * blk:(g + 1) * blk].astype(o_ref.dtype)


def _latent_attention(q, k, v, k_ctx, v_ctx, sink, seq_len, name):
    m, q_w = q.shape
    kv_w = k.shape[1]
    n_batch, past, _ = k_ctx.shape
    blk = WINDOW_BLOCK
    nb = seq_len // blk

    def band_spec(offset):
        def index_map(b, i):
            return (b * nb + jnp.clip(i + offset, 0, nb - 1), 0)
        return pl.BlockSpec((blk, kv_w), index_map)

    ctx_spec = pl.BlockSpec((1, past, kv_w), lambda b, i: (b, 0, 0))
    return pl.pallas_call(
        _lat_attn_kernel,
        grid=(n_batch, nb),
        in_specs=[pl.BlockSpec(memory_space=pltpu.SMEM),
                  pl.BlockSpec((blk, q_w), lambda b, i: (b * nb + i, 0)),
                  band_spec(-1), band_spec(0), band_spec(1),
                  band_spec(-1), band_spec(0), band_spec(1),
                  ctx_spec, ctx_spec],
        out_specs=pl.BlockSpec((blk, q_w), lambda b, i: (b * nb + i, 0)),
        out_shape=jax.ShapeDtypeStruct((m, q_w), BF16),
        scratch_shapes=[pltpu.VMEM((past, kv_w), BF16), pltpu.VMEM((past, kv_w), BF16)],
        compiler_params=_params("parallel", "arbitrary"),
        name=name,
    )(sink, q, k, k, k, v, v, v, k_ctx, v_ctx)


def _rope_tables(seq_len):
    rows = seq_len // GRID_W
    row = jnp.repeat(jnp.arange(rows), GRID_W).astype(F32)
    col = jnp.tile(jnp.arange(GRID_W), rows).astype(F32)
    n_freq = HEAD_DIM // 4
    inv = ROPE_THETA ** (-jnp.arange(n_freq, dtype=F32) / n_freq)
    cos_r, sin_r = jnp.cos(row[:, None] * inv), jnp.sin(row[:, None] * inv)
    cos_c, sin_c = jnp.cos(col[:, None] * inv), jnp.sin(col[:, None] * inv)
    cos = jnp.concatenate([cos_r, cos_r, cos_c, cos_c], axis=-1)
    sin = jnp.concatenate([-sin_r, sin_r, -sin_c, sin_c], axis=-1)
    return cos, sin


def _layer(x, mod, mod_map, seq_len, weights, sink, tag, latent_ctx=None):
    (w_in, w_sconv, w_attn_o, w_conv_o, w_mix_out, g_pre_mix, g_post_mix, g_pre_ffn,
     w_ffn_up, w_ffn_conv, w_ffn_down, g_post_ffn) = weights
    m, d = x.shape
    q_w = N_HEADS * HEAD_DIM
    kv_w = N_KV_HEADS * HEAD_DIM
    conv_dim = w_sconv.shape[1]
    latent = latent_ctx is not None
    rope = _rope_tables(seq_len) if latent else None
    kv_dtype = BF16 if latent else F32

    h = _prenorm(x, g_pre_mix, mod, mod_map, f"prenorm_{tag}")

    k0 = q_w
    v0 = k0 + kv_w
    b0 = v0 + kv_w
    c0 = b0 + conv_dim
    h0 = c0 + conv_dim
    ga0 = h0 + conv_dim
    gc0 = ga0 + d
    q = _linear(h, w_in, 0, q_w, BF16, TM_QKV, TN_WIDE, f"q_proj_{tag}", rope)
    k = _linear(h, w_in, k0 // TN_WIDE, kv_w, kv_dtype, TM_QKV, TN_WIDE, f"k_proj_{tag}", rope)
    v = _linear(h, w_in, v0 // TN_WIDE, kv_w, kv_dtype, TM_QKV, TN_WIDE, f"v_proj_{tag}")

    if latent:
        a = _latent_attention(q, k, v, latent_ctx[0], latent_ctx[1], sink, seq_len, f"attn_{tag}")
    else:
        a = _context_attention(q, k, v, sink, seq_len, f"attn_{tag}")

    sc = _short_conv(h, w_in, [b0 // TN_NARROW, c0 // TN_NARROW, h0 // TN_NARROW],
                     w_sconv, seq_len, f"short_conv_{tag}")
    merged = _merge(h, a, sc, w_in, (ga0 // TN_NARROW, gc0 // TN_NARROW), w_attn_o, w_conv_o,
                    f"merge_{tag}")
    x1, h2 = _linear_post(_mix_post_kernel, merged, w_mix_out, x, [g_post_mix, g_pre_ffn], mod,
                          mod_map, [F32, BF16], TN_WIDE, pl.BlockSpec, f"mix_out_{tag}")

    f = _ffn_up(h2, w_ffn_up, w_ffn_conv, seq_len, f"ffn_up_{tag}")
    (out,) = _linear_post(_down_post_kernel, f, w_ffn_down, x1, [g_post_ffn], mod, mod_map,
                          [F32], TN_NARROW, _resident, f"ffn_down_{tag}")
    return out, k, v


def kernel(x_prompt, x_sample, cache_k, cache_v, c, c_ctx, w_mod, b_mod, g_pre_mix, w_in, w_sconv, attn_sink, w_attn_o, w_conv_o, w_mix_out, g_post_mix, g_pre_ffn, w_ffn_up, w_ffn_conv, w_ffn_down, g_post_ffn):
    batch, seq, d = x_prompt.shape
    dec_batch, dec_seq, _ = x_sample.shape
    depth = w_in.shape[0]
    past = cache_k.shape[2]
    kv_w = N_KV_HEADS * HEAD_DIM
    assert 1 + dec_batch <= MOD_ROWS

    cond = jnp.concatenate([c_ctx[None, :], c, jnp.zeros((MOD_ROWS - 1 - dec_batch, d), F32)], axis=0)

    def ctx_mod_map(i, tm):
        return 0

    def lat_mod_map(i, tm):
        return 1 + i // (dec_seq // tm)

    xp = x_prompt.reshape(batch * seq, d)
    xs = x_sample.reshape(dec_batch * dec_seq, d)
    new_k, new_v = [], []
    for l in range(depth):
        weights = (w_in[l].astype(BF16), w_sconv[l], w_attn_o[l].astype(BF16),
                   w_conv_o[l].astype(BF16), w_mix_out[l].astype(BF16),
                   g_pre_mix[l][None, :], g_post_mix[l][None, :], g_pre_ffn[l][None, :],
                   w_ffn_up[l].astype(BF16), w_ffn_conv[l], w_ffn_down[l].astype(BF16),
                   g_post_ffn[l][None, :])
        mod = _modulation(cond, w_mod[l], b_mod[l]).reshape(MOD_ROWS, N_MOD, d)
        k_ctx = cache_k[:, l].reshape(dec_batch, past, kv_w)
        v_ctx = cache_v[:, l].reshape(dec_batch, past, kv_w)
        xp, k_l, v_l = _layer(xp, mod, ctx_mod_map, seq, weights, attn_sink[l], f"ctx{l}")
        xs, _, _ = _layer(xs, mod, lat_mod_map, dec_seq, weights, attn_sink[l], f"lat{l}",
                          latent_ctx=(k_ctx, v_ctx))
        new_k.append(k_l.reshape(batch, seq, N_KV_HEADS, HEAD_DIM))
        new_v.append(v_l.reshape(batch, seq, N_KV_HEADS, HEAD_DIM))
    return (xp.reshape(batch, seq, d), xs.reshape(dec_batch, dec_seq, d),
            jnp.stack(new_k, axis=1), jnp.stack(new_v, axis=1))
```

```python
import functools
import math

import jax
import jax.numpy as jnp
from jax import lax
from jax.experimental import pallas as pl
from jax.experimental.pallas import tpu as pltpu

F32 = jnp.float32
BF16 = jnp.bfloat16

HEAD_DIM = 128
N_HEADS = 32
N_KV_HEADS = 8
Q_PER_KV = N_HEADS // N_KV_HEADS
WINDOW_BLOCK = 128
GRID_W = 64
ROPE_THETA = 10000.0
EPS = 1e-6
N_MOD = 6
NEG = -1e30
SM_SCALE = HEAD_DIM ** -0.5
LOG2_E = math.log2(math.e)

TM_PROJ = 2048
TM_MERGE = 1024
TM_QKV = 1024
TM_DOWN = 1024
DOT_ROWS = 512
ROPE_DOT_ROWS = 256
TN_WIDE = 512
TN_NARROW = 256
TM_ROWWISE = 256
TN_MOD = 512
MOD_ROWS = 16

VMEM_LIMIT_BYTES = 56 * 1024 * 1024


def _params(*semantics):
    return pltpu.CompilerParams(dimension_semantics=semantics,
                                vmem_limit_bytes=VMEM_LIMIT_BYTES)


def _dot(a, b):
    return jnp.dot(a, b, preferred_element_type=F32)


def _dot_nt(a, b):
    return lax.dot_general(a, b, (((1,), (1,)), ((), ())), preferred_element_type=F32)


def _resident(shape, index_map):
    return pl.BlockSpec(shape, index_map, pipeline_mode=pl.Buffered(1))


def _mod_kernel(cond_ref, w_ref, b_ref, o_ref):
    c = cond_ref[...]
    s = (c * jax.nn.sigmoid(c)).astype(BF16)
    o_ref[...] = _dot(s, w_ref[...].astype(BF16)) + b_ref[...]


def _modulation(cond, w_mod, b_mod):
    rows, d = cond.shape
    n = w_mod.shape[1]
    return pl.pallas_call(
        _mod_kernel,
        grid=(n // TN_MOD,),
        in_specs=[pl.BlockSpec((rows, d), lambda j: (0, 0)),
                  pl.BlockSpec((d, TN_MOD), lambda j: (0, j)),
                  pl.BlockSpec((1, TN_MOD), lambda j: (0, j))],
        out_specs=pl.BlockSpec((rows, TN_MOD), lambda j: (0, j)),
        out_shape=jax.ShapeDtypeStruct((rows, n), F32),
        compiler_params=_params("parallel"),
        name="modulation",
    )(cond, w_mod, b_mod.reshape(1, n))


def _rms(x, g):
    return x * lax.rsqrt(jnp.mean(x * x, axis=-1, keepdims=True) + EPS) * g


def _modulate(y, mod_ref, shift_idx, scale_idx):
    return y * (1 + mod_ref[0, scale_idx:scale_idx + 1, :]) + mod_ref[0, shift_idx:shift_idx + 1, :]


def _prenorm_kernel(x_ref, g_ref, mod_ref, o_ref):
    o_ref[...] = _modulate(_rms(x_ref[...], g_ref[...]), mod_ref, 0, 1).astype(o_ref.dtype)


def _post_mix_kernel(y_ref, x_ref, g_post_ref, g_pre_ref, mod_ref, x1_ref, h2_ref):
    x1 = x_ref[...] + mod_ref[0, 2:3, :] * _rms(y_ref[...], g_post_ref[...])
    x1_ref[...] = x1
    h2_ref[...] = _modulate(_rms(x1, g_pre_ref[...]), mod_ref, 3, 4).astype(h2_ref.dtype)


def _post_ffn_kernel(y_ref, x_ref, g_post_ref, mod_ref, o_ref):
    o_ref[...] = x_ref[...] + mod_ref[0, 5:6, :] * _rms(y_ref[...], g_post_ref[...])


def _rowwise_call(kernel, row_inputs, gains, mod, mod_map, out_dtypes, name):
    m, d = row_inputs[0].shape
    row_spec = pl.BlockSpec((TM_ROWWISE, d), lambda i: (i, 0))
    gain_spec = pl.BlockSpec((1, d), lambda i: (0, 0))
    mod_spec = pl.BlockSpec((1, N_MOD, d), lambda i: (mod_map(i, TM_ROWWISE), 0, 0))
    return pl.pallas_call(
        kernel,
        grid=(m // TM_ROWWISE,),
        in_specs=[row_spec] * len(row_inputs) + [gain_spec] * len(gains) + [mod_spec],
        out_specs=[row_spec] * len(out_dtypes),
        out_shape=[jax.ShapeDtypeStruct((m, d), dt) for dt in out_dtypes],
        compiler_params=_params("parallel"),
        name=name,
    )(*row_inputs, *gains, mod)


def _linear_kernel(h_ref, w_ref, o_ref):
    o_ref[...] = _dot(h_ref[...], w_ref[...]).astype(o_ref.dtype)


def _linear(h, w, col_block0, n_out, out_dtype, tm, tn, name, lhs_spec=pl.BlockSpec):
    m, k = h.shape
    return pl.pallas_call(
        _linear_kernel,
        grid=(m // tm, n_out // tn),
        in_specs=[lhs_spec((tm, k), lambda i, j: (i, 0)),
                  pl.BlockSpec((k, tn), lambda i, j: (0, col_block0 + j))],
        out_specs=pl.BlockSpec((tm, tn), lambda i, j: (i, j)),
        out_shape=jax.ShapeDtypeStruct((m, n_out), out_dtype),
        compiler_params=_params("parallel", "arbitrary"),
        name=name,
    )(h, w)


def _dot_row_chunks(h_ref, w):
    rows = h_ref.shape[0]
    return jnp.concatenate([_dot(h_ref[r:r + DOT_ROWS, :], w) for r in range(0, rows, DOT_ROWS)],
                           axis=0)


def _merge_kernel(h_ref, a_ref, sc_ref, wga_ref, wgc_ref, wo_ref, wc_ref, o_ref):
    wga, wgc, wo, wc = wga_ref[...], wgc_ref[...], wo_ref[...], wc_ref[...]
    for r in range(0, h_ref.shape[0], DOT_ROWS):
        rows = slice(r, r + DOT_ROWS)
        h = h_ref[rows, :]
        att = jax.nn.sigmoid(_dot(h, wga)) * _dot(a_ref[rows, :], wo)
        conv = jax.nn.sigmoid(_dot(h, wgc)) * _dot(sc_ref[rows, :], wc)
        o_ref[rows, :] = (att + conv).astype(o_ref.dtype)


def _merge(h, a, sc, w_in, gate_col_blocks, w_attn_o, w_conv_o, name):
    m, d = h.shape
    tm, tn = TM_MERGE, TN_NARROW
    ga0, gc0 = gate_col_blocks
    return pl.pallas_call(
        _merge_kernel,
        grid=(m // tm, d // tn),
        in_specs=[_resident((tm, d), lambda i, j: (i, 0)),
                  _resident((tm, a.shape[1]), lambda i, j: (i, 0)),
                  _resident((tm, sc.shape[1]), lambda i, j: (i, 0)),
                  pl.BlockSpec((d, tn), lambda i, j: (0, ga0 + j)),
                  pl.BlockSpec((d, tn), lambda i, j: (0, gc0 + j)),
                  pl.BlockSpec((a.shape[1], tn), lambda i, j: (0, j)),
                  pl.BlockSpec((sc.shape[1], tn), lambda i, j: (0, j))],
        out_specs=pl.BlockSpec((tm, tn), lambda i, j: (i, j)),
        out_shape=jax.ShapeDtypeStruct((m, d), BF16),
        compiler_params=_params("parallel", "arbitrary"),
        name=name,
    )(h, a, sc, w_in, w_in, w_attn_o, w_conv_o)


def _rope_kernel(h_ref, w_ref, cos_ref, sin_ref, o_ref):
    w = w_ref[...]
    lane = lax.broadcasted_iota(jnp.int32, (ROPE_DOT_ROWS, HEAD_DIM), 1)
    first_half = (lane % (HEAD_DIM // 2)) < (HEAD_DIM // 4)
    for r in range(0, h_ref.shape[0], ROPE_DOT_ROWS):
        rows = slice(r, r + ROPE_DOT_ROWS)
        y = _dot(h_ref[rows, :], w)
        cos = cos_ref[rows, :]
        sin = sin_ref[rows, :]
        for hd in range(y.shape[1] // HEAD_DIM):
            cols = slice(hd * HEAD_DIM, (hd + 1) * HEAD_DIM)
            x = y[:, cols]
            partner = jnp.where(first_half,
                                pltpu.roll(x, HEAD_DIM - HEAD_DIM // 4, axis=1),
                                pltpu.roll(x, HEAD_DIM // 4, axis=1))
            o_ref[rows, cols] = (x * cos + partner * sin).astype(o_ref.dtype)


def _rope_proj(h, w, col_block0, n_out, rope, tm, tn, name):
    m, k = h.shape
    tiles_per_seq = rope[0].shape[0] // tm
    table_spec = pl.BlockSpec((tm, HEAD_DIM), lambda i, j: (i % tiles_per_seq, 0))
    return pl.pallas_call(
        _rope_kernel,
        grid=(m // tm, n_out // tn),
        in_specs=[pl.BlockSpec((tm, k), lambda i, j: (i, 0)),
                  pl.BlockSpec((k, tn), lambda i, j: (0, col_block0 + j)),
                  table_spec, table_spec],
        out_specs=pl.BlockSpec((tm, tn), lambda i, j: (i, j)),
        out_shape=jax.ShapeDtypeStruct((m, n_out), BF16),
        compiler_params=_params("parallel", "arbitrary"),
        name=name,
    )(h, w, *rope)


def _dwconv3_rows(x, w, seq_len):
    rows = x.shape[0]
    assert seq_len & (seq_len - 1) == 0 and rows % seq_len == 0
    t = lax.broadcasted_iota(jnp.int32, (rows, 1), 0) & (seq_len - 1)
    prev = jnp.where(t == 0, 0.0, pltpu.roll(x, 1, axis=0))
    nxt = jnp.where(t == seq_len - 1, 0.0, pltpu.roll(x, rows - 1, axis=0))
    return prev * w[0:1, :] + x * w[1:2, :] + nxt * w[2:3, :]


def _sconv_kernel(h_ref, wb_ref, wc_ref, wh_ref, cw_ref, o_ref, *, seq_len):
    conv = _dwconv3_rows(_dot_row_chunks(h_ref, wc_ref[...]) * _dot_row_chunks(h_ref, wh_ref[...]),
                         cw_ref[...], seq_len)
    o_ref[...] = (_dot_row_chunks(h_ref, wb_ref[...]) * conv).astype(o_ref.dtype)


def _short_conv(h, w_in, col_blocks, conv_w, seq_len, name):
    m, k = h.shape
    width = conv_w.shape[1]
    tm, tn = TM_PROJ, TN_NARROW
    w_specs = [pl.BlockSpec((k, tn), functools.partial(lambda i, j, c0: (0, c0 + j), c0=c0))
               for c0 in col_blocks]
    return pl.pallas_call(
        functools.partial(_sconv_kernel, seq_len=seq_len),
        grid=(m // tm, width // tn),
        in_specs=[_resident((tm, k), lambda i, j: (i, 0))] + w_specs
                 + [pl.BlockSpec((conv_w.shape[0], tn), lambda i, j: (0, j))],
        out_specs=pl.BlockSpec((tm, tn), lambda i, j: (i, j)),
        out_shape=jax.ShapeDtypeStruct((m, width), BF16),
        compiler_params=_params("parallel", "arbitrary"),
        name=name,
    )(h, w_in, w_in, w_in, conv_w)


def _ffn_up_kernel(h_ref, wg_ref, wv_ref, cw_ref, o_ref, *, seq_len):
    gate = _dwconv3_rows(_dot_row_chunks(h_ref, wg_ref[...]), cw_ref[...], seq_len)
    o_ref[...] = (gate * jax.nn.sigmoid(gate) * _dot_row_chunks(h_ref, wv_ref[...])).astype(o_ref.dtype)


def _ffn_up(h, w_up, conv_w, seq_len, name):
    m, k = h.shape
    d_ff = conv_w.shape[1]
    tm, tn = TM_PROJ, TN_NARROW
    val0 = d_ff // tn
    return pl.pallas_call(
        functools.partial(_ffn_up_kernel, seq_len=seq_len),
        grid=(m // tm, d_ff // tn),
        in_specs=[_resident((tm, k), lambda i, j: (i, 0)),
                  pl.BlockSpec((k, tn), lambda i, j: (0, j)),
                  pl.BlockSpec((k, tn), lambda i, j: (0, val0 + j)),
                  pl.BlockSpec((conv_w.shape[0], tn), lambda i, j: (0, j))],
        out_specs=pl.BlockSpec((tm, tn), lambda i, j: (i, j)),
        out_shape=jax.ShapeDtypeStruct((m, d_ff), BF16),
        compiler_params=_params("parallel", "arbitrary"),
        name=name,
    )(h, w_up, w_up, conv_w)


def _with_ones_column(v):
    lane = lax.broadcasted_iota(jnp.int32, v.shape, 1)
    return jnp.concatenate([v, (lane == 0).astype(v.dtype)], axis=1)


def _sink_attend(raw_scores, values, sink):
    raw_max = raw_scores[0].max(axis=-1, keepdims=True)
    for s in raw_scores[1:]:
        raw_max = jnp.maximum(raw_max, s.max(axis=-1, keepdims=True))
    m = jnp.maximum(raw_max * SM_SCALE, sink)
    m_log2 = m * LOG2_E
    acc = None
    for s, v in zip(raw_scores, values):
        p = jnp.exp2(s * (SM_SCALE * LOG2_E) - m_log2)
        pv = _dot(p.astype(BF16), _with_ones_column(v))
        acc = pv if acc is None else acc + pv
    denom = acc[:, HEAD_DIM:HEAD_DIM + 1] + jnp.exp(sink - m)
    return acc[:, :HEAD_DIM] / denom


def _group_queries(q_ref, sink_ref, kv, rows):
    heads = [kv * Q_PER_KV + g for g in range(Q_PER_KV)]
    q = jnp.concatenate([q_ref[:, h * HEAD_DIM:(h + 1) * HEAD_DIM] for h in heads], axis=0)
    sink = jnp.concatenate([jnp.full((rows, 1), sink_ref[h], F32) for h in heads], axis=0)
    return heads, q, sink


def _ctx_attn_kernel(sink_ref, q_ref, k_ref, v_ref, o_ref):
    rows = q_ref.shape[0]
    for kv in range(N_KV_HEADS):
        cols = slice(kv * HEAD_DIM, (kv + 1) * HEAD_DIM)
        heads, q, sink = _group_queries(q_ref, sink_ref, kv, rows)
        s = _dot_nt(q, k_ref[:, cols].astype(BF16))
        o = _sink_attend([s], [v_ref[:, cols].astype(BF16)], sink)
        for g, h in enumerate(heads):
            o_ref[:, h * HEAD_DIM:(h + 1) * HEAD_DIM] = o[g * rows:(g + 1) * rows].astype(o_ref.dtype)


def _context_attention(q, k, v, sink, seq_len, name):
    m = q.shape[0]
    return pl.pallas_call(
        _ctx_attn_kernel,
        grid=(m // seq_len,),
        in_specs=[pl.BlockSpec(memory_space=pltpu.SMEM),
                  pl.BlockSpec((seq_len, q.shape[1]), lambda b: (b, 0)),
                  pl.BlockSpec((seq_len, k.shape[1]), lambda b: (b, 0)),
                  pl.BlockSpec((seq_len, v.shape[1]), lambda b: (b, 0))],
        out_specs=pl.BlockSpec((seq_len, q.shape[1]), lambda b: (b, 0)),
        out_shape=jax.ShapeDtypeStruct(q.shape, BF16),
        compiler_params=_params("parallel"),
        name=name,
    )(sink, q, k, v)


def _lat_attn_kernel(sink_ref, q_ref, kp_ref, kc_ref, kn_ref, vp_ref, vc_ref, vn_ref,
                     kx_ref, vx_ref, o_ref, kx_bf, vx_bf):
    blk = WINDOW_BLOCK
    qb = pl.program_id(1)
    n_blocks = pl.num_programs(1)

    @pl.when(qb == 0)
    def _():
        kx_bf[...] = kx_ref[0].astype(BF16)
        vx_bf[...] = vx_ref[0].astype(BF16)

    shape = (Q_PER_KV * blk, 3 * blk)
    r = lax.broadcasted_iota(jnp.int32, shape, 0) & (blk - 1)
    c = lax.broadcasted_iota(jnp.int32, shape, 1)
    in_window = (c >= r) & (c <= r + 2 * blk)
    in_sequence = ((c >= blk) | (qb > 0)) & ((c < 2 * blk) | (qb < n_blocks - 1))
    band_mask = in_window & in_sequence

    for kv in range(N_KV_HEADS):
        cols = slice(kv * HEAD_DIM, (kv + 1) * HEAD_DIM)
        heads, q, sink = _group_queries(q_ref, sink_ref, kv, blk)
        k_band = jnp.concatenate([kp_ref[:, cols], kc_ref[:, cols], kn_ref[:, cols]], axis=0)
        v_band = jnp.concatenate([vp_ref[:, cols], vc_ref[:, cols], vn_ref[:, cols]], axis=0)
        s_band = jnp.where(band_mask, _dot_nt(q, k_band), NEG)
        s_ctx = _dot_nt(q, kx_bf[:, cols])
        o = _sink_attend([s_band, s_ctx], [v_band, vx_bf[:, cols]], sink)
        for g, h in enumerate(heads):
            o_ref[:, h * HEAD_DIM:(h + 1) * HEAD_DIM] = o[g * blk:(g + 1) * blk].astype(o_ref.dtype)


def _latent_attention(q, k, v, k_ctx, v_ctx, sink, seq_len, name):
    m, q_w = q.shape
    kv_w = k.shape[1]
    n_batch, past, _ = k_ctx.shape
    blk = WINDOW_BLOCK
    nb = seq_len // blk

    def band_spec(offset):
        def index_map(b, i):
            return (b * nb + jnp.clip(i + offset, 0, nb - 1), 0)
        return pl.BlockSpec((blk, kv_w), index_map)

    ctx_spec = pl.BlockSpec((1, past, kv_w), lambda b, i: (b, 0, 0))
    return pl.pallas_call(
        _lat_attn_kernel,
        grid=(n_batch, nb),
        in_specs=[pl.BlockSpec(memory_space=pltpu.SMEM),
                  pl.BlockSpec((blk, q_w), lambda b, i: (b * nb + i, 0)),
                  band_spec(-1), band_spec(0), band_spec(1),
                  band_spec(-1), band_spec(0), band_spec(1),
                  ctx_spec, ctx_spec],
        out_specs=pl.BlockSpec((blk, q_w), lambda b, i: (b * nb + i, 0)),
        out_shape=jax.ShapeDtypeStruct((m, q_w), BF16),
        scratch_shapes=[pltpu.VMEM((past, kv_w), BF16), pltpu.VMEM((past, kv_w), BF16)],
        compiler_params=_params("parallel", "arbitrary"),
        name=name,
    )(sink, q, k, k, k, v, v, v, k_ctx, v_ctx)


def _rope_tables(seq_len):
    rows = seq_len // GRID_W
    row = jnp.repeat(jnp.arange(rows), GRID_W).astype(F32)
    col = jnp.tile(jnp.arange(GRID_W), rows).astype(F32)
    n_freq = HEAD_DIM // 4
    inv = ROPE_THETA ** (-jnp.arange(n_freq, dtype=F32) / n_freq)
    cos_r, sin_r = jnp.cos(row[:, None] * inv), jnp.sin(row[:, None] * inv)
    cos_c, sin_c = jnp.cos(col[:, None] * inv), jnp.sin(col[:, None] * inv)
    cos = jnp.concatenate([cos_r, cos_r, cos_c, cos_c], axis=-1)
    sin = jnp.concatenate([-sin_r, sin_r, -sin_c, sin_c], axis=-1)
    return cos, sin


def _layer(x, mod, mod_map, seq_len, weights, sink, tag, latent_ctx=None):
    (w_in, w_sconv, w_attn_o, w_conv_o, w_mix_out, g_pre_mix, g_post_mix, g_pre_ffn,
     w_ffn_up, w_ffn_conv, w_ffn_down, g_post_ffn) = weights
    m, d = x.shape
    q_w = N_HEADS * HEAD_DIM
    kv_w = N_KV_HEADS * HEAD_DIM
    conv_dim = w_sconv.shape[1]
    latent = latent_ctx is not None
    rope = _rope_tables(seq_len) if latent else None

    (h,) = _rowwise_call(_prenorm_kernel, [x], [g_pre_mix], mod, mod_map, [BF16], f"prenorm_{tag}")

    k0 = q_w
    v0 = k0 + kv_w
    b0 = v0 + kv_w
    c0 = b0 + conv_dim
    h0 = c0 + conv_dim
    ga0 = h0 + conv_dim
    gc0 = ga0 + d
    if latent:
        q = _rope_proj(h, w_in, 0, q_w, rope, TM_QKV, TN_WIDE, f"q_proj_{tag}")
        k = _rope_proj(h, w_in, k0 // TN_WIDE, kv_w, rope, TM_QKV, TN_WIDE, f"k_proj_{tag}")
        v = _linear(h, w_in, v0 // TN_WIDE, kv_w, BF16, TM_QKV, TN_WIDE, f"v_proj_{tag}")
        a = _latent_attention(q, k, v, latent_ctx[0], latent_ctx[1], sink, seq_len, f"attn_{tag}")
    else:
        q = _linear(h, w_in, 0, q_w, BF16, TM_QKV, TN_WIDE, f"q_proj_{tag}")
        k = _linear(h, w_in, k0 // TN_WIDE, kv_w, F32, TM_QKV, TN_WIDE, f"k_proj_{tag}")
        v = _linear(h, w_in, v0 // TN_WIDE, kv_w, F32, TM_QKV, TN_WIDE, f"v_proj_{tag}")
        a = _context_attention(q, k, v, sink, seq_len, f"attn_{tag}")

    sc = _short_conv(h, w_in, [b0 // TN_NARROW, c0 // TN_NARROW, h0 // TN_NARROW],
                     w_sconv, seq_len, f"short_conv_{tag}")
    merged = _merge(h, a, sc, w_in, (ga0 // TN_NARROW, gc0 // TN_NARROW), w_attn_o, w_conv_o,
                    f"merge_{tag}")
    y = _linear(merged, w_mix_out, 0, d, F32, TM_QKV, TN_WIDE, f"mix_out_{tag}")
    x1, h2 = _rowwise_call(_post_mix_kernel, [y, x], [g_post_mix, g_pre_ffn], mod, mod_map,
                           [F32, BF16], f"post_mix_{tag}")

    f = _ffn_up(h2, w_ffn_up, w_ffn_conv, seq_len, f"ffn_up_{tag}")
    y2 = _linear(f, w_ffn_down, 0, d, F32, TM_DOWN, TN_NARROW, f"ffn_down_{tag}", _resident)
    (out,) = _rowwise_call(_post_ffn_kernel, [y2, x1], [g_post_ffn], mod, mod_map, [F32],
                           f"post_ffn_{tag}")
    return out, k, v


def kernel(x_prompt, x_sample, cache_k, cache_v, c, c_ctx, w_mod, b_mod, g_pre_mix, w_in, w_sconv, attn_sink, w_attn_o, w_conv_o, w_mix_out, g_post_mix, g_pre_ffn, w_ffn_up, w_ffn_conv, w_ffn_down, g_post_ffn):
    batch, seq, d = x_prompt.shape
    dec_batch, dec_seq, _ = x_sample.shape
    depth = w_in.shape[0]
    past = cache_k.shape[2]
    kv_w = N_KV_HEADS * HEAD_DIM
    assert 1 + dec_batch <= MOD_ROWS

    cond = jnp.concatenate([c_ctx[None, :], c, jnp.zeros((MOD_ROWS - 1 - dec_batch, d), F32)], axis=0)

    def ctx_mod_map(i, tm):
        return 0

    def lat_mod_map(i, tm):
        return 1 + i // (dec_seq // tm)

    xp = x_prompt.reshape(batch * seq, d)
    xs = x_sample.reshape(dec_batch * dec_seq, d)
    new_k, new_v = [], []
    for l in range(depth):
        weights = (w_in[l].astype(BF16), w_sconv[l], w_attn_o[l].astype(BF16),
                   w_conv_o[l].astype(BF16), w_mix_out[l].astype(BF16),
                   g_pre_mix[l][None, :], g_post_mix[l][None, :], g_pre_ffn[l][None, :],
                   w_ffn_up[l].astype(BF16), w_ffn_conv[l], w_ffn_down[l].astype(BF16),
                   g_post_ffn[l][None, :])
        mod = _modulation(cond, w_mod[l], b_mod[l]).reshape(MOD_ROWS, N_MOD, d)
        k_ctx = cache_k[:, l].reshape(dec_batch, past, kv_w)
        v_ctx = cache_v[:, l].reshape(dec_batch, past, kv_w)
        xp, k_l, v_l = _layer(xp, mod, ctx_mod_map, seq, weights, attn_sink[l], f"ctx{l}")
        xs, _, _ = _layer(xs, mod, lat_mod_map, dec_seq, weights, attn_sink[l], f"lat{l}",
                          latent_ctx=(k_ctx, v_ctx))
        new_k.append(k_l.reshape(batch, seq, N_KV_HEADS, HEAD_DIM))
        new_v.append(v_l.reshape(batch, seq, N_KV_HEADS, HEAD_DIM))
    return (xp.reshape(batch, seq, d), xs.reshape(dec_batch, dec_seq, d),
            jnp.stack(new_k, axis=1), jnp.stack(new_v, axis=1))
```

```python
import functools
import math

import jax
import jax.numpy as jnp
from jax import lax
from jax.experimental import pallas as pl
from jax.experimental.pallas import tpu as pltpu

F32 = jnp.float32
BF16 = jnp.bfloat16

HEAD_DIM = 128
N_HEADS = 32
N_KV_HEADS = 8
Q_PER_KV = N_HEADS // N_KV_HEADS
WINDOW_BLOCK = 128
GRID_W = 64
ROPE_THETA = 10000.0
EPS = 1e-6
N_MOD = 6
NEG = -1e30
SM_SCALE = HEAD_DIM ** -0.5
LOG2_E = math.log2(math.e)

TM_PROJ = 2048
TM_MERGE = 1024
TM_QKV = 1024
TM_DOWN = 1024
DOT_ROWS = 512
ROPE_DOT_ROWS = 256
SINK_KEY_ROWS = 128
TN_LINEAR = 1024
TN_WIDE = 512
TN_NARROW = 256
TM_ROWWISE = 256
TN_MOD = 512
BF16_SUBLANES = 16
MOD_ROWS = BF16_SUBLANES

VMEM_LIMIT_BYTES = 56 * 1024 * 1024


def _params(*semantics):
    return pltpu.CompilerParams(dimension_semantics=semantics,
                                vmem_limit_bytes=VMEM_LIMIT_BYTES)


def _dot(a, b):
    return jnp.dot(a, b, preferred_element_type=F32)


def _dot_nt(a, b):
    return lax.dot_general(a, b, (((1,), (1,)), ((), ())), preferred_element_type=F32)


def _resident(shape, index_map):
    return pl.BlockSpec(shape, index_map, pipeline_mode=pl.Buffered(1))


def _mod_kernel(cond_ref, w_ref, b_ref, o_ref):
    c = cond_ref[...]
    s = (c * jax.nn.sigmoid(c)).astype(BF16)
    o_ref[...] = _dot(s, w_ref[...].astype(BF16)) + b_ref[...]


def _modulation(cond, w_mod, b_mod):
    rows, d = cond.shape
    n = w_mod.shape[1]
    return pl.pallas_call(
        _mod_kernel,
        grid=(n // TN_MOD,),
        in_specs=[pl.BlockSpec((rows, d), lambda j: (0, 0)),
                  pl.BlockSpec((d, TN_MOD), lambda j: (0, j)),
                  pl.BlockSpec((1, TN_MOD), lambda j: (0, j))],
        out_specs=pl.BlockSpec((rows, TN_MOD), lambda j: (0, j)),
        out_shape=jax.ShapeDtypeStruct((rows, n), F32),
        compiler_params=_params("parallel"),
        name="modulation",
    )(cond, w_mod, b_mod.reshape(1, n))


def _rms(x, g):
    return x * lax.rsqrt(jnp.mean(x * x, axis=-1, keepdims=True) + EPS) * g


def _modulate(y, mod_ref, shift_idx, scale_idx):
    return y * (1 + mod_ref[0, scale_idx:scale_idx + 1, :]) + mod_ref[0, shift_idx:shift_idx + 1, :]


def _prenorm_kernel(x_ref, g_ref, mod_ref, o_ref):
    o_ref[...] = _modulate(_rms(x_ref[...], g_ref[...]), mod_ref, 0, 1).astype(o_ref.dtype)


def _post_mix_kernel(y_ref, x_ref, g_post_ref, g_pre_ref, mod_ref, x1_ref, h2_ref):
    x1 = x_ref[...] + mod_ref[0, 2:3, :] * _rms(y_ref[...], g_post_ref[...])
    x1_ref[...] = x1
    h2_ref[...] = _modulate(_rms(x1, g_pre_ref[...]), mod_ref, 3, 4).astype(h2_ref.dtype)


def _post_ffn_kernel(y_ref, x_ref, g_post_ref, mod_ref, o_ref):
    o_ref[...] = x_ref[...] + mod_ref[0, 5:6, :] * _rms(y_ref[...], g_post_ref[...])


def _rowwise_call(kernel, row_inputs, gains, mod, mod_map, out_dtypes, name):
    m, d = row_inputs[0].shape
    row_spec = pl.BlockSpec((TM_ROWWISE, d), lambda i: (i, 0))
    gain_spec = pl.BlockSpec((1, d), lambda i: (0, 0))
    mod_spec = pl.BlockSpec((1, N_MOD, d), lambda i: (mod_map(i, TM_ROWWISE), 0, 0))
    return pl.pallas_call(
        kernel,
        grid=(m // TM_ROWWISE,),
        in_specs=[row_spec] * len(row_inputs) + [gain_spec] * len(gains) + [mod_spec],
        out_specs=[row_spec] * len(out_dtypes),
        out_shape=[jax.ShapeDtypeStruct((m, d), dt) for dt in out_dtypes],
        compiler_params=_params("parallel"),
        name=name,
    )(*row_inputs, *gains, mod)


def _linear_kernel(h_ref, w_ref, o_ref):
    o_ref[...] = _dot(h_ref[...], w_ref[...]).astype(o_ref.dtype)


def _linear(h, w, col_block0, n_out, out_dtype, tm, tn, name, lhs_spec=pl.BlockSpec):
    m, k = h.shape
    return pl.pallas_call(
        _linear_kernel,
        grid=(m // tm, n_out // tn),
        in_specs=[lhs_spec((tm, k), lambda i, j: (i, 0)),
                  pl.BlockSpec((k, tn), lambda i, j: (0, col_block0 + j))],
        out_specs=pl.BlockSpec((tm, tn), lambda i, j: (i, j)),
        out_shape=jax.ShapeDtypeStruct((m, n_out), out_dtype),
        compiler_params=_params("parallel", "arbitrary"),
        name=name,
    )(h, w)


def _dot_row_chunks(h_ref, w):
    rows = h_ref.shape[0]
    return jnp.concatenate([_dot(h_ref[r:r + DOT_ROWS, :], w) for r in range(0, rows, DOT_ROWS)],
                           axis=0)


def _merge_kernel(h_ref, a_ref, sc_ref, wga_ref, wgc_ref, wo_ref, wc_ref, o_ref):
    wga, wgc, wo, wc = wga_ref[...], wgc_ref[...], wo_ref[...], wc_ref[...]
    for r in range(0, h_ref.shape[0], DOT_ROWS):
        rows = slice(r, r + DOT_ROWS)
        h = h_ref[rows, :]
        att = jax.nn.sigmoid(_dot(h, wga)) * _dot(a_ref[rows, :], wo)
        conv = jax.nn.sigmoid(_dot(h, wgc)) * _dot(sc_ref[rows, :], wc)
        o_ref[rows, :] = (att + conv).astype(o_ref.dtype)


def _with_cast_rider(kernel, n_in):
    def wrapped(*refs):
        rider_in, o_ref, rider_out = refs[n_in], refs[n_in + 1], refs[n_in + 2]
        rider_out[...] = rider_in[...].astype(rider_out.dtype)
        kernel(*refs[:n_in], o_ref)
    return wrapped


def _projection_call(kernel, grid, in_specs, args, out_spec, out_shape, name, cast_rider=None):
    if cast_rider is None:
        return pl.pallas_call(kernel, grid=grid, in_specs=in_specs, out_specs=out_spec,
                              out_shape=out_shape, name=name,
                              compiler_params=_params("parallel", "arbitrary"))(*args)
    n_steps = grid[0] * grid[1]
    rows = cast_rider.shape[0] // n_steps
    assert rows * n_steps == cast_rider.shape[0] and rows % BF16_SUBLANES == 0
    rider_spec = pl.BlockSpec((rows, cast_rider.shape[1]), lambda i, j: (i * grid[1] + j, 0))
    return pl.pallas_call(
        _with_cast_rider(kernel, len(args)), grid=grid,
        in_specs=in_specs + [rider_spec], out_specs=[out_spec, rider_spec],
        out_shape=[out_shape, jax.ShapeDtypeStruct(cast_rider.shape, BF16)], name=name,
        compiler_params=_params("arbitrary", "arbitrary"))(*args, cast_rider)


def _merge(h, a, sc, w_in, gate_col_blocks, w_attn_o, w_conv_o, name, cast_rider=None):
    m, d = h.shape
    tm, tn = TM_MERGE, TN_NARROW
    ga0, gc0 = gate_col_blocks
    return _projection_call(
        _merge_kernel, (m // tm, d // tn),
        [_resident((tm, d), lambda i, j: (i, 0)),
         _resident((tm, a.shape[1]), lambda i, j: (i, 0)),
         _resident((tm, sc.shape[1]), lambda i, j: (i, 0)),
         pl.BlockSpec((d, tn), lambda i, j: (0, ga0 + j)),
         pl.BlockSpec((d, tn), lambda i, j: (0, gc0 + j)),
         pl.BlockSpec((a.shape[1], tn), lambda i, j: (0, j)),
         pl.BlockSpec((sc.shape[1], tn), lambda i, j: (0, j))],
        (h, a, sc, w_in, w_in, w_attn_o, w_conv_o),
        pl.BlockSpec((tm, tn), lambda i, j: (i, j)), jax.ShapeDtypeStruct((m, d), BF16),
        name, cast_rider)


def _rope_kernel(h_ref, w_ref, cos_ref, sin_ref, o_ref):
    w = w_ref[...]
    lane = lax.broadcasted_iota(jnp.int32, (ROPE_DOT_ROWS, HEAD_DIM), 1)
    first_half = (lane % (HEAD_DIM // 2)) < (HEAD_DIM // 4)
    for r in range(0, h_ref.shape[0], ROPE_DOT_ROWS):
        rows = slice(r, r + ROPE_DOT_ROWS)
        y = _dot(h_ref[rows, :], w)
        cos = cos_ref[rows, :]
        sin = sin_ref[rows, :]
        for hd in range(y.shape[1] // HEAD_DIM):
            cols = slice(hd * HEAD_DIM, (hd + 1) * HEAD_DIM)
            x = y[:, cols]
            partner = jnp.where(first_half,
                                pltpu.roll(x, HEAD_DIM - HEAD_DIM // 4, axis=1),
                                pltpu.roll(x, HEAD_DIM // 4, axis=1))
            o_ref[rows, cols] = (x * cos + partner * sin).astype(o_ref.dtype)


def _rope_proj(h, w, col_block0, n_out, rope, tm, tn, name):
    m, k = h.shape
    tiles_per_seq = rope[0].shape[0] // tm
    table_spec = pl.BlockSpec((tm, HEAD_DIM), lambda i, j: (i % tiles_per_seq, 0))
    return pl.pallas_call(
        _rope_kernel,
        grid=(m // tm, n_out // tn),
        in_specs=[pl.BlockSpec((tm, k), lambda i, j: (i, 0)),
                  pl.BlockSpec((k, tn), lambda i, j: (0, col_block0 + j)),
                  table_spec, table_spec],
        out_specs=pl.BlockSpec((tm, tn), lambda i, j: (i, j)),
        out_shape=jax.ShapeDtypeStruct((m, n_out), BF16),
        compiler_params=_params("parallel", "arbitrary"),
        name=name,
    )(h, w, *rope)


def _dwconv3_rows(x, w, seq_len):
    rows = x.shape[0]
    assert seq_len & (seq_len - 1) == 0 and rows % seq_len == 0
    t = lax.broadcasted_iota(jnp.int32, (rows, 1), 0) & (seq_len - 1)
    prev = jnp.where(t == 0, 0.0, pltpu.roll(x, 1, axis=0))
    nxt = jnp.where(t == seq_len - 1, 0.0, pltpu.roll(x, rows - 1, axis=0))
    return prev * w[0:1, :] + x * w[1:2, :] + nxt * w[2:3, :]


def _sconv_kernel(h_ref, wb_ref, wc_ref, wh_ref, cw_ref, o_ref, *, seq_len):
    conv = _dwconv3_rows(_dot_row_chunks(h_ref, wc_ref[...]) * _dot_row_chunks(h_ref, wh_ref[...]),
                         cw_ref[...], seq_len)
    o_ref[...] = (_dot_row_chunks(h_ref, wb_ref[...]) * conv).astype(o_ref.dtype)


def _short_conv(h, w_in, col_blocks, conv_w, seq_len, name):
    m, k = h.shape
    width = conv_w.shape[1]
    tm, tn = TM_PROJ, TN_NARROW
    w_specs = [pl.BlockSpec((k, tn), functools.partial(lambda i, j, c0: (0, c0 + j), c0=c0))
               for c0 in col_blocks]
    return pl.pallas_call(
        functools.partial(_sconv_kernel, seq_len=seq_len),
        grid=(m // tm, width // tn),
        in_specs=[_resident((tm, k), lambda i, j: (i, 0))] + w_specs
                 + [pl.BlockSpec((conv_w.shape[0], tn), lambda i, j: (0, j))],
        out_specs=pl.BlockSpec((tm, tn), lambda i, j: (i, j)),
        out_shape=jax.ShapeDtypeStruct((m, width), BF16),
        compiler_params=_params("parallel", "arbitrary"),
        name=name,
    )(h, w_in, w_in, w_in, conv_w)


def _ffn_up_kernel(h_ref, wg_ref, wv_ref, cw_ref, o_ref, *, seq_len):
    gate = _dwconv3_rows(_dot_row_chunks(h_ref, wg_ref[...]), cw_ref[...], seq_len)
    o_ref[...] = (gate * jax.nn.sigmoid(gate) * _dot_row_chunks(h_ref, wv_ref[...])).astype(o_ref.dtype)


def _ffn_up(h, w_up, conv_w, seq_len, name, cast_rider=None):
    m, k = h.shape
    d_ff = conv_w.shape[1]
    tm, tn = TM_PROJ, TN_NARROW
    val0 = d_ff // tn
    return _projection_call(
        functools.partial(_ffn_up_kernel, seq_len=seq_len), (m // tm, d_ff // tn),
        [_resident((tm, k), lambda i, j: (i, 0)),
         pl.BlockSpec((k, tn), lambda i, j: (0, j)),
         pl.BlockSpec((k, tn), lambda i, j: (0, val0 + j)),
         pl.BlockSpec((conv_w.shape[0], tn), lambda i, j: (0, j))],
        (h, w_up, w_up, conv_w),
        pl.BlockSpec((tm, tn), lambda i, j: (i, j)), jax.ShapeDtypeStruct((m, d_ff), BF16),
        name, cast_rider)


def _sink_attend_t(raw_scores_t, values, sink):
    n_q = sink.shape[1]
    raw_max = raw_scores_t[0].max(axis=0, keepdims=True)
    for s in raw_scores_t[1:]:
        raw_max = jnp.maximum(raw_max, s.max(axis=0, keepdims=True))
    m = jnp.maximum(raw_max * SM_SCALE, sink)
    m_log2 = m * LOG2_E
    probs_t = [jnp.exp2(s * (SM_SCALE * LOG2_E) - m_log2).astype(BF16) for s in raw_scores_t]
    sink_row = lax.broadcasted_iota(jnp.int32, (SINK_KEY_ROWS, n_q), 0) == 0
    probs_t.append(jnp.where(sink_row, jnp.exp(sink - m), 0.0).astype(BF16))
    p = jnp.concatenate(probs_t, axis=0).T
    v = jnp.concatenate(list(values) + [jnp.zeros((SINK_KEY_ROWS, HEAD_DIM), BF16)], axis=0)
    ones_col = (lax.broadcasted_iota(jnp.int32, v.shape, 1) == 0).astype(BF16)
    acc = _dot(p, jnp.concatenate([v, ones_col], axis=1))
    return acc[:, :HEAD_DIM] / acc[:, HEAD_DIM:HEAD_DIM + 1]


def _group_queries(q_ref, sink_ref, kv, rows):
    heads = [kv * Q_PER_KV + g for g in range(Q_PER_KV)]
    q = jnp.concatenate([q_ref[:, h * HEAD_DIM:(h + 1) * HEAD_DIM] for h in heads], axis=0)
    sink = jnp.concatenate([jnp.full((1, rows), sink_ref[h], F32) for h in heads], axis=1)
    return heads, q, sink


def _store_heads(o_ref, o, heads, rows):
    for g, h in enumerate(heads):
        o_ref[:, h * HEAD_DIM:(h + 1) * HEAD_DIM] = o[g * rows:(g + 1) * rows, :].astype(o_ref.dtype)


def _ctx_attn_kernel(sink_ref, q_ref, k_ref, v_ref, o_ref):
    rows = q_ref.shape[0]
    for kv in range(N_KV_HEADS):
        cols = slice(kv * HEAD_DIM, (kv + 1) * HEAD_DIM)
        heads, q, sink = _group_queries(q_ref, sink_ref, kv, rows)
        s_t = _dot_nt(k_ref[:, cols].astype(BF16), q)
        o = _sink_attend_t([s_t], [v_ref[:, cols].astype(BF16)], sink)
        _store_heads(o_ref, o, heads, rows)


def _context_attention(q, k, v, sink, seq_len, name):
    m = q.shape[0]
    return pl.pallas_call(
        _ctx_attn_kernel,
        grid=(m // seq_len,),
        in_specs=[pl.BlockSpec(memory_space=pltpu.SMEM),
                  pl.BlockSpec((seq_len, q.shape[1]), lambda b: (b, 0)),
                  pl.BlockSpec((seq_len, k.shape[1]), lambda b: (b, 0)),
                  pl.BlockSpec((seq_len, v.shape[1]), lambda b: (b, 0))],
        out_specs=pl.BlockSpec((seq_len, q.shape[1]), lambda b: (b, 0)),
        out_shape=jax.ShapeDtypeStruct(q.shape, BF16),
        compiler_params=_params("parallel"),
        name=name,
    )(sink, q, k, v)


def _lat_attn_kernel(sink_ref, q_ref, kp_ref, kc_ref, kn_ref, vp_ref, vc_ref, vn_ref,
                     kx_ref, vx_ref, o_ref, kx_bf, vx_bf):
    blk = WINDOW_BLOCK
    qb = pl.program_id(1)
    n_blocks = pl.num_programs(1)

    @pl.when(qb == 0)
    def _():
        kx_bf[...] = kx_ref[0].astype(BF16)
        vx_bf[...] = vx_ref[0].astype(BF16)

    shape = (3 * blk, Q_PER_KV * blk)
    c = lax.broadcasted_iota(jnp.int32, shape, 0)
    r = lax.broadcasted_iota(jnp.int32, shape, 1) & (blk - 1)
    in_window = (c >= r) & (c <= r + 2 * blk)
    in_sequence = ((c >= blk) | (qb > 0)) & ((c < 2 * blk) | (qb < n_blocks - 1))
    band_mask = in_window & in_sequence

    for kv in range(N_KV_HEADS):
        cols = slice(kv * HEAD_DIM, (kv + 1) * HEAD_DIM)
        heads, q, sink = _group_queries(q_ref, sink_ref, kv, blk)
        k_band = jnp.concatenate([kp_ref[:, cols], kc_ref[:, cols], kn_ref[:, cols]], axis=0)
        v_band = jnp.concatenate([vp_ref[:, cols], vc_ref[:, cols], vn_ref[:, cols]], axis=0)
        s_band_t = jnp.where(band_mask, _dot_nt(k_band, q), NEG)
        s_ctx_t = _dot_nt(kx_bf[:, cols], q)
        o = _sink_attend_t([s_band_t, s_ctx_t], [v_band, vx_bf[:, cols]], sink)
        _store_heads(o_ref, o, heads, blk)


def _latent_attention(q, k, v, k_ctx, v_ctx, sink, seq_len, name):
    m, q_w = q.shape
    kv_w = k.shape[1]
    n_batch, past, _ = k_ctx.shape
    blk = WINDOW_BLOCK
    nb = seq_len // blk

    def band_block(b, i, offset):
        return b * nb + jnp.clip(i + offset, 0, nb - 1)

    def band_spec(offset):
        return pl.BlockSpec((blk, kv_w), lambda b, i: (band_block(b, i, offset), 0))

    ctx_spec = pl.BlockSpec((1, past, kv_w), lambda b, i: (b, 0, 0))
    return pl.pallas_call(
        _lat_attn_kernel,
        grid=(n_batch, nb),
        in_specs=[pl.BlockSpec(memory_space=pltpu.SMEM),
                  pl.BlockSpec((blk, q_w), lambda b, i: (b * nb + i, 0)),
                  band_spec(-1), band_spec(0), band_spec(1),
                  band_spec(-1), band_spec(0), band_spec(1),
                  ctx_spec, ctx_spec],
        out_specs=pl.BlockSpec((blk, q_w), lambda b, i: (b * nb + i, 0)),
        out_shape=jax.ShapeDtypeStruct((m, q_w), BF16),
        scratch_shapes=[pltpu.VMEM((past, kv_w), BF16), pltpu.VMEM((past, kv_w), BF16)],
        compiler_params=_params("parallel", "arbitrary"),
        name=name,
    )(sink, q, k, k, k, v, v, v, k_ctx, v_ctx)


def _rope_tables(seq_len):
    rows = seq_len // GRID_W
    row = jnp.repeat(jnp.arange(rows), GRID_W).astype(F32)
    col = jnp.tile(jnp.arange(GRID_W), rows).astype(F32)
    n_freq = HEAD_DIM // 4
    inv = ROPE_THETA ** (-jnp.arange(n_freq, dtype=F32) / n_freq)
    cos_r, sin_r = jnp.cos(row[:, None] * inv), jnp.sin(row[:, None] * inv)
    cos_c, sin_c = jnp.cos(col[:, None] * inv), jnp.sin(col[:, None] * inv)
    cos = jnp.concatenate([cos_r, cos_r, cos_c, cos_c], axis=-1)
    sin = jnp.concatenate([-sin_r, sin_r, -sin_c, sin_c], axis=-1)
    return cos, sin


def _layer(x, mod, mod_map, seq_len, weights, sink, tag, latent_ctx=None):
    (w_in, w_sconv, w_attn_o, w_conv_o, w_mix_out, g_pre_mix, g_post_mix, g_pre_ffn,
     w_ffn_up, w_ffn_conv, w_ffn_down, g_post_ffn) = weights
    m, d = x.shape
    q_w = N_HEADS * HEAD_DIM
    kv_w = N_KV_HEADS * HEAD_DIM
    conv_dim = w_sconv.shape[1]
    latent = latent_ctx is not None
    rope = _rope_tables(seq_len) if latent else None

    (h,) = _rowwise_call(_prenorm_kernel, [x], [g_pre_mix], mod, mod_map, [BF16], f"prenorm_{tag}")

    k0 = q_w
    v0 = k0 + kv_w
    b0 = v0 + kv_w
    c0 = b0 + conv_dim
    h0 = c0 + conv_dim
    ga0 = h0 + conv_dim
    gc0 = ga0 + d
    if latent:
        q = _rope_proj(h, w_in, 0, q_w, rope, TM_QKV, TN_WIDE, f"q_proj_{tag}")
        k = _rope_proj(h, w_in, k0 // TN_WIDE, kv_w, rope, TM_QKV, TN_WIDE, f"k_proj_{tag}")
        v = _linear(h, w_in, v0 // TN_LINEAR, kv_w, BF16, TM_QKV, TN_LINEAR, f"v_proj_{tag}")
        a = _latent_attention(q, k, v, latent_ctx[0], latent_ctx[1], sink, seq_len, f"attn_{tag}")
    else:
        q = _linear(h, w_in, 0, q_w, BF16, TM_QKV, TN_LINEAR, f"q_proj_{tag}")
        k = _linear(h, w_in, k0 // TN_LINEAR, kv_w, F32, TM_QKV, TN_LINEAR, f"k_proj_{tag}")
        v = _linear(h, w_in, v0 // TN_LINEAR, kv_w, F32, TM_QKV, TN_LINEAR, f"v_proj_{tag}")
        a = _context_attention(q, k, v, sink, seq_len, f"attn_{tag}")

    sc = _short_conv(h, w_in, [b0 // TN_NARROW, c0 // TN_NARROW, h0 // TN_NARROW],
                     w_sconv, seq_len, f"short_conv_{tag}")
    merged = _merge(h, a, sc, w_in, (ga0 // TN_NARROW, gc0 // TN_NARROW), w_attn_o, w_conv_o,
                    f"merge_{tag}", cast_rider=w_ffn_up if w_ffn_up.dtype == F32 else None)
    if w_ffn_up.dtype == F32:
        merged, w_ffn_up = merged
    y = _linear(merged, w_mix_out, 0, d, F32, TM_QKV, TN_LINEAR, f"mix_out_{tag}")
    x1, h2 = _rowwise_call(_post_mix_kernel, [y, x], [g_post_mix, g_pre_ffn], mod, mod_map,
                           [F32, BF16], f"post_mix_{tag}")

    f = _ffn_up(h2, w_ffn_up, w_ffn_conv, seq_len, f"ffn_up_{tag}",
                cast_rider=w_ffn_down if w_ffn_down.dtype == F32 else None)
    if w_ffn_down.dtype == F32:
        f, w_ffn_down = f
    y2 = _linear(f, w_ffn_down, 0, d, F32, TM_DOWN, TN_WIDE, f"ffn_down_{tag}", _resident)
    (out,) = _rowwise_call(_post_ffn_kernel, [y2, x1], [g_post_ffn], mod, mod_map, [F32],
                           f"post_ffn_{tag}")
    return out, k, v, (w_ffn_up, w_ffn_down)


def kernel(x_prompt, x_sample, cache_k, cache_v, c, c_ctx, w_mod, b_mod, g_pre_mix, w_in, w_sconv, attn_sink, w_attn_o, w_conv_o, w_mix_out, g_post_mix, g_pre_ffn, w_ffn_up, w_ffn_conv, w_ffn_down, g_post_ffn):
    batch, seq, d = x_prompt.shape
    dec_batch, dec_seq, _ = x_sample.shape
    depth = w_in.shape[0]
    past = cache_k.shape[2]
    kv_w = N_KV_HEADS * HEAD_DIM
    assert 1 + dec_batch <= MOD_ROWS

    cond = jnp.concatenate([c_ctx[None, :], c, jnp.zeros((MOD_ROWS - 1 - dec_batch, d), F32)], axis=0)

    def ctx_mod_map(i, tm):
        return 0

    def lat_mod_map(i, tm):
        return 1 + i // (dec_seq // tm)

    xp = x_prompt.reshape(batch * seq, d)
    xs = x_sample.reshape(dec_batch * dec_seq, d)
    new_k, new_v = [], []
    for l in range(depth):
        def layer_weights(up, down):
            return (w_in_bf, w_sconv[l], w_attn_o_bf, w_conv_o_bf, w_mix_out_bf,
                    g_pre_mix[l][None, :], g_post_mix[l][None, :], g_pre_ffn[l][None, :],
                    up, w_ffn_conv[l], down, g_post_ffn[l][None, :])

        w_in_bf = w_in[l].astype(BF16)
        w_attn_o_bf = w_attn_o[l].astype(BF16)
        w_conv_o_bf = w_conv_o[l].astype(BF16)
        w_mix_out_bf = w_mix_out[l].astype(BF16)
        mod = _modulation(cond, w_mod[l], b_mod[l]).reshape(MOD_ROWS, N_MOD, d)
        k_ctx = cache_k[:, l].reshape(dec_batch, past, kv_w)
        v_ctx = cache_v[:, l].reshape(dec_batch, past, kv_w)
        xp, k_l, v_l, ffn_bf = _layer(xp, mod, ctx_mod_map, seq,
                                      layer_weights(w_ffn_up[l], w_ffn_down[l]),
                                      attn_sink[l], f"ctx{l}")
        xs, _, _, _ = _layer(xs, mod, lat_mod_map, dec_seq, layer_weights(*ffn_bf), attn_sink[l],
                             f"lat{l}", latent_ctx=(k_ctx, v_ctx))
        new_k.append(k_l.reshape(batch, seq, N_KV_HEADS, HEAD_DIM))
        new_v.append(v_l.reshape(batch, seq, N_KV_HEADS, HEAD_DIM))
    return (xp.reshape(batch, seq, d), xs.reshape(dec_batch, dec_seq, d),
            jnp.stack(new_k, axis=1), jnp.stack(new_v, axis=1))
```

```python
import functools
import math
from typing import Callable, NamedTuple

import jax
import jax.numpy as jnp
from jax import lax
from jax.experimental import pallas as pl
from jax.experimental.pallas import tpu as pltpu

F32 = jnp.float32
BF16 = jnp.bfloat16

HEAD_DIM = 128
N_HEADS = 32
N_KV_HEADS = 8
Q_PER_KV = N_HEADS // N_KV_HEADS
WINDOW_BLOCK = 128
GRID_W = 64
ROPE_THETA = 10000.0
EPS = 1e-6
N_MOD = 6
NEG = -1e30
SM_SCALE = HEAD_DIM ** -0.5
LOG2_E = math.log2(math.e)

TM_PROJ = 2048
TM_MERGE = 1024
TM_QKV = 1024
TM_DOWN = 1024
DOT_ROWS = 512
ROPE_DOT_ROWS = 256
SINK_KEY_ROWS = 128
TN_LINEAR = 1024
TN_WIDE = 512
TN_NARROW = 256
TM_ROWWISE = 256
TN_MOD = 512
BF16_SUBLANES = 16
MOD_ROWS = BF16_SUBLANES

VMEM_LIMIT_BYTES = 56 * 1024 * 1024


def _params(*semantics):
    return pltpu.CompilerParams(dimension_semantics=semantics,
                                vmem_limit_bytes=VMEM_LIMIT_BYTES)


def _dot(a, b):
    return jnp.dot(a, b, preferred_element_type=F32)


def _dot_nt(a, b):
    return lax.dot_general(a, b, (((1,), (1,)), ((), ())), preferred_element_type=F32)


def _resident(shape, index_map):
    return pl.BlockSpec(shape, index_map, pipeline_mode=pl.Buffered(1))


def _mod_kernel(cond_ref, w_ref, b_ref, o_ref):
    c = cond_ref[...]
    s = (c * jax.nn.sigmoid(c)).astype(BF16)
    o_ref[...] = _dot(s, w_ref[...].astype(BF16)) + b_ref[...]


def _modulation(cond, w_mod, b_mod):
    rows, d = cond.shape
    n = w_mod.shape[1]
    return pl.pallas_call(
        _mod_kernel,
        grid=(n // TN_MOD,),
        in_specs=[pl.BlockSpec((rows, d), lambda j: (0, 0)),
                  pl.BlockSpec((d, TN_MOD), lambda j: (0, j)),
                  pl.BlockSpec((1, TN_MOD), lambda j: (0, j))],
        out_specs=pl.BlockSpec((rows, TN_MOD), lambda j: (0, j)),
        out_shape=jax.ShapeDtypeStruct((rows, n), F32),
        compiler_params=_params("parallel"),
        name="modulation",
    )(cond, w_mod, b_mod.reshape(1, n))


def _rms(x, g):
    return x * lax.rsqrt(jnp.mean(x * x, axis=-1, keepdims=True) + EPS) * g


def _modulate(y, mod_ref, shift_idx, scale_idx):
    return y * (1 + mod_ref[0, scale_idx:scale_idx + 1, :]) + mod_ref[0, shift_idx:shift_idx + 1, :]


def _prenorm_kernel(x_ref, g_ref, mod_ref, o_ref):
    o_ref[...] = _modulate(_rms(x_ref[...], g_ref[...]), mod_ref, 0, 1).astype(o_ref.dtype)


def _post_mix_kernel(y_ref, x_ref, g_post_ref, g_pre_ref, mod_ref, x1_ref, h2_ref):
    x1 = x_ref[...] + mod_ref[0, 2:3, :] * _rms(y_ref[...], g_post_ref[...])
    x1_ref[...] = x1
    h2_ref[...] = _modulate(_rms(x1, g_pre_ref[...]), mod_ref, 3, 4).astype(h2_ref.dtype)


def _post_ffn_kernel(y_ref, x_ref, g_post_ref, mod_ref, o_ref):
    o_ref[...] = x_ref[...] + mod_ref[0, 5:6, :] * _rms(y_ref[...], g_post_ref[...])


def _rowwise_call(kernel, row_inputs, gains, mod, mod_map, out_dtypes, name):
    m, d = row_inputs[0].shape
    row_spec = pl.BlockSpec((TM_ROWWISE, d), lambda i: (i, 0))
    gain_spec = pl.BlockSpec((1, d), lambda i: (0, 0))
    mod_spec = pl.BlockSpec((1, N_MOD, d), lambda i: (mod_map(i, TM_ROWWISE), 0, 0))
    return pl.pallas_call(
        kernel,
        grid=(m // TM_ROWWISE,),
        in_specs=[row_spec] * len(row_inputs) + [gain_spec] * len(gains) + [mod_spec],
        out_specs=[row_spec] * len(out_dtypes),
        out_shape=[jax.ShapeDtypeStruct((m, d), dt) for dt in out_dtypes],
        compiler_params=_params("parallel"),
        name=name,
    )(*row_inputs, *gains, mod)


def _linear_kernel(h_ref, w_ref, o_ref):
    o_ref[...] = _dot(h_ref[...], w_ref[...]).astype(o_ref.dtype)


def _linear(h, w, col_block0, n_out, out_dtype, tm, tn, name, lhs_spec=pl.BlockSpec, rider=None):
    m, k = h.shape
    return _projection_call(
        _linear_kernel, (m // tm, n_out // tn),
        [lhs_spec((tm, k), lambda i, j: (i, 0)),
         pl.BlockSpec((k, tn), lambda i, j: (0, col_block0 + j))],
        (h, w),
        pl.BlockSpec((tm, tn), lambda i, j: (i, j)), jax.ShapeDtypeStruct((m, n_out), out_dtype),
        name, rider)


def _dot_row_chunks(h_ref, w):
    rows = h_ref.shape[0]
    return jnp.concatenate([_dot(h_ref[r:r + DOT_ROWS, :], w) for r in range(0, rows, DOT_ROWS)],
                           axis=0)


def _merge_kernel(h_ref, a_ref, sc_ref, wga_ref, wgc_ref, wo_ref, wc_ref, o_ref):
    wga, wgc, wo, wc = wga_ref[...], wgc_ref[...], wo_ref[...], wc_ref[...]
    for r in range(0, h_ref.shape[0], DOT_ROWS):
        rows = slice(r, r + DOT_ROWS)
        h = h_ref[rows, :]
        att = jax.nn.sigmoid(_dot(h, wga)) * _dot(a_ref[rows, :], wo)
        conv = jax.nn.sigmoid(_dot(h, wgc)) * _dot(sc_ref[rows, :], wc)
        o_ref[rows, :] = (att + conv).astype(o_ref.dtype)


class Rider(NamedTuple):
    make: Callable


def _row_block(m, n_steps):
    rows = m // n_steps
    assert rows * n_steps == m and rows % BF16_SUBLANES == 0
    return rows


def _cast_kernel(w_ref, o_ref):
    o_ref[...] = w_ref[...].astype(o_ref.dtype)


def _cast_rider(w):
    def make(n_steps):
        block = ((_row_block(w.shape[0], n_steps), w.shape[1]), lambda s: (s, 0))
        return _cast_kernel, [(w, *block)], [(jax.ShapeDtypeStruct(w.shape, BF16), *block)]
    return Rider(make)


def _rowwise_rider(kernel, row_inputs, gains, mod, mod_map, out_dtypes):
    def make(n_steps):
        m, d = row_inputs[0].shape
        rows = _row_block(m, n_steps)
        block = ((rows, d), lambda s: (s, 0))
        inputs = ([(a, *block) for a in row_inputs]
                  + [(g, (1, d), lambda s: (0, 0)) for g in gains]
                  + [(mod, (1, N_MOD, d), lambda s: (mod_map(s, rows), 0, 0))])
        return kernel, inputs, [(jax.ShapeDtypeStruct((m, d), dt), *block) for dt in out_dtypes]
    return Rider(make)


def _projection_call(kernel, grid, in_specs, args, out_spec, out_shape, name, rider=None):
    if rider is None:
        return pl.pallas_call(kernel, grid=grid, in_specs=in_specs, out_specs=out_spec,
                              out_shape=out_shape, name=name,
                              compiler_params=_params("parallel", "arbitrary"))(*args)
    rider_kernel, r_in, r_out = rider.make(grid[0] * grid[1])
    n_in, n_rin, n_rout = len(args), len(r_in), len(r_out)

    def with_step(index_map):
        return lambda i, j: index_map(i * grid[1] + j)

    def hosted(*refs):
        rider_kernel(*refs[n_in:n_in + n_rin], *refs[n_in + n_rin + 1:])
        kernel(*refs[:n_in], refs[n_in + n_rin])

    outs = pl.pallas_call(
        hosted, grid=grid,
        in_specs=in_specs + [pl.BlockSpec(shape, with_step(f)) for _, shape, f in r_in],
        out_specs=[out_spec] + [pl.BlockSpec(shape, with_step(f)) for _, shape, f in r_out],
        out_shape=[out_shape] + [s for s, _, _ in r_out], name=name,
        compiler_params=_params("arbitrary", "arbitrary"))(*args, *[a for a, _, _ in r_in])
    return outs[0], outs[1:]


def _merge(h, a, sc, w_in, gate_col_blocks, w_attn_o, w_conv_o, name, rider=None):
    m, d = h.shape
    tm, tn = TM_MERGE, TN_NARROW
    ga0, gc0 = gate_col_blocks
    return _projection_call(
        _merge_kernel, (m // tm, d // tn),
        [_resident((tm, d), lambda i, j: (i, 0)),
         _resident((tm, a.shape[1]), lambda i, j: (i, 0)),
         _resident((tm, sc.shape[1]), lambda i, j: (i, 0)),
         pl.BlockSpec((d, tn), lambda i, j: (0, ga0 + j)),
         pl.BlockSpec((d, tn), lambda i, j: (0, gc0 + j)),
         pl.BlockSpec((a.shape[1], tn), lambda i, j: (0, j)),
         pl.BlockSpec((sc.shape[1], tn), lambda i, j: (0, j))],
        (h, a, sc, w_in, w_in, w_attn_o, w_conv_o),
        pl.BlockSpec((tm, tn), lambda i, j: (i, j)), jax.ShapeDtypeStruct((m, d), BF16),
        name, rider)


def _rope_kernel(h_ref, w_ref, cos_ref, sin_ref, o_ref):
    w = w_ref[...]
    lane = lax.broadcasted_iota(jnp.int32, (ROPE_DOT_ROWS, HEAD_DIM), 1)
    first_half = (lane % (HEAD_DIM // 2)) < (HEAD_DIM // 4)
    for r in range(0, h_ref.shape[0], ROPE_DOT_ROWS):
        rows = slice(r, r + ROPE_DOT_ROWS)
        y = _dot(h_ref[rows, :], w)
        cos = cos_ref[rows, :]
        sin = sin_ref[rows, :]
        for hd in range(y.shape[1] // HEAD_DIM):
            cols = slice(hd * HEAD_DIM, (hd + 1) * HEAD_DIM)
            x = y[:, cols]
            partner = jnp.where(first_half,
                                pltpu.roll(x, HEAD_DIM - HEAD_DIM // 4, axis=1),
                                pltpu.roll(x, HEAD_DIM // 4, axis=1))
            o_ref[rows, cols] = (x * cos + partner * sin).astype(o_ref.dtype)


def _rope_proj(h, w, col_block0, n_out, rope, tm, tn, name):
    m, k = h.shape
    tiles_per_seq = rope[0].shape[0] // tm
    table_spec = pl.BlockSpec((tm, HEAD_DIM), lambda i, j: (i % tiles_per_seq, 0))
    return pl.pallas_call(
        _rope_kernel,
        grid=(m // tm, n_out // tn),
        in_specs=[pl.BlockSpec((tm, k), lambda i, j: (i, 0)),
                  pl.BlockSpec((k, tn), lambda i, j: (0, col_block0 + j)),
                  table_spec, table_spec],
        out_specs=pl.BlockSpec((tm, tn), lambda i, j: (i, j)),
        out_shape=jax.ShapeDtypeStruct((m, n_out), BF16),
        compiler_params=_params("parallel", "arbitrary"),
        name=name,
    )(h, w, *rope)


def _dwconv3_rows(x, w, seq_len):
    rows = x.shape[0]
    assert seq_len & (seq_len - 1) == 0 and rows % seq_len == 0
    t = lax.broadcasted_iota(jnp.int32, (rows, 1), 0) & (seq_len - 1)
    prev = jnp.where(t == 0, 0.0, pltpu.roll(x, 1, axis=0))
    nxt = jnp.where(t == seq_len - 1, 0.0, pltpu.roll(x, rows - 1, axis=0))
    return prev * w[0:1, :] + x * w[1:2, :] + nxt * w[2:3, :]


def _sconv_kernel(h_ref, wb_ref, wc_ref, wh_ref, cw_ref, o_ref, *, seq_len):
    conv = _dwconv3_rows(_dot_row_chunks(h_ref, wc_ref[...]) * _dot_row_chunks(h_ref, wh_ref[...]),
                         cw_ref[...], seq_len)
    o_ref[...] = (_dot_row_chunks(h_ref, wb_ref[...]) * conv).astype(o_ref.dtype)


def _short_conv(h, w_in, col_blocks, conv_w, seq_len, name):
    m, k = h.shape
    width = conv_w.shape[1]
    tm, tn = TM_PROJ, TN_NARROW
    w_specs = [pl.BlockSpec((k, tn), functools.partial(lambda i, j, c0: (0, c0 + j), c0=c0))
               for c0 in col_blocks]
    return pl.pallas_call(
        functools.partial(_sconv_kernel, seq_len=seq_len),
        grid=(m // tm, width // tn),
        in_specs=[_resident((tm, k), lambda i, j: (i, 0))] + w_specs
                 + [pl.BlockSpec((conv_w.shape[0], tn), lambda i, j: (0, j))],
        out_specs=pl.BlockSpec((tm, tn), lambda i, j: (i, j)),
        out_shape=jax.ShapeDtypeStruct((m, width), BF16),
        compiler_params=_params("parallel", "arbitrary"),
        name=name,
    )(h, w_in, w_in, w_in, conv_w)


def _ffn_up_kernel(h_ref, wg_ref, wv_ref, cw_ref, o_ref, *, seq_len):
    gate = _dwconv3_rows(_dot_row_chunks(h_ref, wg_ref[...]), cw_ref[...], seq_len)
    o_ref[...] = (gate * jax.nn.sigmoid(gate) * _dot_row_chunks(h_ref, wv_ref[...])).astype(o_ref.dtype)


def _ffn_up(h, w_up, conv_w, seq_len, name, rider=None):
    m, k = h.shape
    d_ff = conv_w.shape[1]
    tm, tn = TM_PROJ, TN_NARROW
    val0 = d_ff // tn
    return _projection_call(
        functools.partial(_ffn_up_kernel, seq_len=seq_len), (m // tm, d_ff // tn),
        [_resident((tm, k), lambda i, j: (i, 0)),
         pl.BlockSpec((k, tn), lambda i, j: (0, j)),
         pl.BlockSpec((k, tn), lambda i, j: (0, val0 + j)),
         pl.BlockSpec((conv_w.shape[0], tn), lambda i, j: (0, j))],
        (h, w_up, w_up, conv_w),
        pl.BlockSpec((tm, tn), lambda i, j: (i, j)), jax.ShapeDtypeStruct((m, d_ff), BF16),
        name, rider)


def _sink_attend_t(raw_scores_t, values, sink):
    n_q = sink.shape[1]
    raw_max = raw_scores_t[0].max(axis=0, keepdims=True)
    for s in raw_scores_t[1:]:
        raw_max = jnp.maximum(raw_max, s.max(axis=0, keepdims=True))
    m = jnp.maximum(raw_max * SM_SCALE, sink)
    m_log2 = m * LOG2_E
    probs_t = [jnp.exp2(s * (SM_SCALE * LOG2_E) - m_log2).astype(BF16) for s in raw_scores_t]
    sink_row = lax.broadcasted_iota(jnp.int32, (SINK_KEY_ROWS, n_q), 0) == 0
    probs_t.append(jnp.where(sink_row, jnp.exp(sink - m), 0.0).astype(BF16))
    p = jnp.concatenate(probs_t, axis=0).T
    v = jnp.concatenate(list(values) + [jnp.zeros((SINK_KEY_ROWS, HEAD_DIM), BF16)], axis=0)
    ones_col = (lax.broadcasted_iota(jnp.int32, v.shape, 1) == 0).astype(BF16)
    acc = _dot(p, jnp.concatenate([v, ones_col], axis=1))
    return acc[:, :HEAD_DIM] / acc[:, HEAD_DIM:HEAD_DIM + 1]


def _group_queries(q_ref, sink_ref, kv, rows):
    heads = [kv * Q_PER_KV + g for g in range(Q_PER_KV)]
    q = jnp.concatenate([q_ref[:, h * HEAD_DIM:(h + 1) * HEAD_DIM] for h in heads], axis=0)
    sink = jnp.concatenate([jnp.full((1, rows), sink_ref[h], F32) for h in heads], axis=1)
    return heads, q, sink


def _store_heads(o_ref, o, heads, rows):
    for g, h in enumerate(heads):
        o_ref[:, h * HEAD_DIM:(h + 1) * HEAD_DIM] = o[g * rows:(g + 1) * rows, :].astype(o_ref.dtype)


def _ctx_attn_kernel(sink_ref, q_ref, k_ref, v_ref, o_ref):
    rows = q_ref.shape[0]
    for kv in range(N_KV_HEADS):
        cols = slice(kv * HEAD_DIM, (kv + 1) * HEAD_DIM)
        heads, q, sink = _group_queries(q_ref, sink_ref, kv, rows)
        s_t = _dot_nt(k_ref[:, cols].astype(BF16), q)
        o = _sink_attend_t([s_t], [v_ref[:, cols].astype(BF16)], sink)
        _store_heads(o_ref, o, heads, rows)


def _context_attention(q, k, v, sink, seq_len, name):
    m = q.shape[0]
    return pl.pallas_call(
        _ctx_attn_kernel,
        grid=(m // seq_len,),
        in_specs=[pl.BlockSpec(memory_space=pltpu.SMEM),
                  pl.BlockSpec((seq_len, q.shape[1]), lambda b: (b, 0)),
                  pl.BlockSpec((seq_len, k.shape[1]), lambda b: (b, 0)),
                  pl.BlockSpec((seq_len, v.shape[1]), lambda b: (b, 0))],
        out_specs=pl.BlockSpec((seq_len, q.shape[1]), lambda b: (b, 0)),
        out_shape=jax.ShapeDtypeStruct(q.shape, BF16),
        compiler_params=_params("parallel"),
        name=name,
    )(sink, q, k, v)


def _lat_attn_kernel(sink_ref, q_ref, kp_ref, kc_ref, kn_ref, vp_ref, vc_ref, vn_ref,
                     kx_ref, vx_ref, o_ref, kx_bf, vx_bf):
    blk = WINDOW_BLOCK
    qb = pl.program_id(1)
    n_blocks = pl.num_programs(1)

    @pl.when(qb == 0)
    def _():
        kx_bf[...] = kx_ref[0].astype(BF16)
        vx_bf[...] = vx_ref[0].astype(BF16)

    shape = (3 * blk, Q_PER_KV * blk)
    c = lax.broadcasted_iota(jnp.int32, shape, 0)
    r = lax.broadcasted_iota(jnp.int32, shape, 1) & (blk - 1)
    in_window = (c >= r) & (c <= r + 2 * blk)
    in_sequence = ((c >= blk) | (qb > 0)) & ((c < 2 * blk) | (qb < n_blocks - 1))
    band_mask = in_window & in_sequence

    for kv in range(N_KV_HEADS):
        cols = slice(kv * HEAD_DIM, (kv + 1) * HEAD_DIM)
        heads, q, sink = _group_queries(q_ref, sink_ref, kv, blk)
        k_band = jnp.concatenate([kp_ref[:, cols], kc_ref[:, cols], kn_ref[:, cols]], axis=0)
        v_band = jnp.concatenate([vp_ref[:, cols], vc_ref[:, cols], vn_ref[:, cols]], axis=0)
        s_band_t = jnp.where(band_mask, _dot_nt(k_band, q), NEG)
        s_ctx_t = _dot_nt(kx_bf[:, cols], q)
        o = _sink_attend_t([s_band_t, s_ctx_t], [v_band, vx_bf[:, cols]], sink)
        _store_heads(o_ref, o, heads, blk)


def _latent_attention(q, k, v, k_ctx, v_ctx, sink, seq_len, name):
    m, q_w = q.shape
    kv_w = k.shape[1]
    n_batch, past, _ = k_ctx.shape
    blk = WINDOW_BLOCK
    nb = seq_len // blk

    def band_block(b, i, offset):
        return b * nb + jnp.clip(i + offset, 0, nb - 1)

    def band_spec(offset):
        return pl.BlockSpec((blk, kv_w), lambda b, i: (band_block(b, i, offset), 0))

    ctx_spec = pl.BlockSpec((1, past, kv_w), lambda b, i: (b, 0, 0))
    return pl.pallas_call(
        _lat_attn_kernel,
        grid=(n_batch, nb),
        in_specs=[pl.BlockSpec(memory_space=pltpu.SMEM),
                  pl.BlockSpec((blk, q_w), lambda b, i: (b * nb + i, 0)),
                  band_spec(-1), band_spec(0), band_spec(1),
                  band_spec(-1), band_spec(0), band_spec(1),
                  ctx_spec, ctx_spec],
        out_specs=pl.BlockSpec((blk, q_w), lambda b, i: (b * nb + i, 0)),
        out_shape=jax.ShapeDtypeStruct((m, q_w), BF16),
        scratch_shapes=[pltpu.VMEM((past, kv_w), BF16), pltpu.VMEM((past, kv_w), BF16)],
        compiler_params=_params("parallel", "arbitrary"),
        name=name,
    )(sink, q, k, k, k, v, v, v, k_ctx, v_ctx)


def _rope_tables(seq_len):
    rows = seq_len // GRID_W
    row = jnp.repeat(jnp.arange(rows), GRID_W).astype(F32)
    col = jnp.tile(jnp.arange(GRID_W), rows).astype(F32)
    n_freq = HEAD_DIM // 4
    inv = ROPE_THETA ** (-jnp.arange(n_freq, dtype=F32) / n_freq)
    cos_r, sin_r = jnp.cos(row[:, None] * inv), jnp.sin(row[:, None] * inv)
    cos_c, sin_c = jnp.cos(col[:, None] * inv), jnp.sin(col[:, None] * inv)
    cos = jnp.concatenate([cos_r, cos_r, cos_c, cos_c], axis=-1)
    sin = jnp.concatenate([-sin_r, sin_r, -sin_c, sin_c], axis=-1)
    return cos, sin


def _mixer(x, mod, mod_map, seq_len, weights, sink, tag, latent_ctx=None, merge_rider=None):
    w_in, w_sconv, w_attn_o, w_conv_o, w_mix_out, g_pre_mix = weights
    m, d = x.shape
    q_w = N_HEADS * HEAD_DIM
    kv_w = N_KV_HEADS * HEAD_DIM
    conv_dim = w_sconv.shape[1]
    latent = latent_ctx is not None
    rope = _rope_tables(seq_len) if latent else None

    (h,) = _rowwise_call(_prenorm_kernel, [x], [g_pre_mix], mod, mod_map, [BF16], f"prenorm_{tag}")

    k0 = q_w
    v0 = k0 + kv_w
    b0 = v0 + kv_w
    c0 = b0 + conv_dim
    h0 = c0 + conv_dim
    ga0 = h0 + conv_dim
    gc0 = ga0 + d
    if latent:
        q = _rope_proj(h, w_in, 0, q_w, rope, TM_QKV, TN_WIDE, f"q_proj_{tag}")
        k = _rope_proj(h, w_in, k0 // TN_WIDE, kv_w, rope, TM_QKV, TN_WIDE, f"k_proj_{tag}")
        v = _linear(h, w_in, v0 // TN_LINEAR, kv_w, BF16, TM_QKV, TN_LINEAR, f"v_proj_{tag}")
        a = _latent_attention(q, k, v, latent_ctx[0], latent_ctx[1], sink, seq_len, f"attn_{tag}")
    else:
        q = _linear(h, w_in, 0, q_w, BF16, TM_QKV, TN_LINEAR, f"q_proj_{tag}")
        k = _linear(h, w_in, k0 // TN_LINEAR, kv_w, F32, TM_QKV, TN_LINEAR, f"k_proj_{tag}")
        v = _linear(h, w_in, v0 // TN_LINEAR, kv_w, F32, TM_QKV, TN_LINEAR, f"v_proj_{tag}")
        a = _context_attention(q, k, v, sink, seq_len, f"attn_{tag}")

    sc = _short_conv(h, w_in, [b0 // TN_NARROW, c0 // TN_NARROW, h0 // TN_NARROW],
                     w_sconv, seq_len, f"short_conv_{tag}")
    merged = _merge(h, a, sc, w_in, (ga0 // TN_NARROW, gc0 // TN_NARROW), w_attn_o, w_conv_o,
                    f"merge_{tag}", merge_rider)
    merged, ridden = merged if merge_rider is not None else (merged, [])
    y = _linear(merged, w_mix_out, 0, d, F32, TM_QKV, TN_LINEAR, f"mix_out_{tag}")
    return y, k, v, ridden


def kernel(x_prompt, x_sample, cache_k, cache_v, c, c_ctx, w_mod, b_mod, g_pre_mix, w_in, w_sconv, attn_sink, w_attn_o, w_conv_o, w_mix_out, g_post_mix, g_pre_ffn, w_ffn_up, w_ffn_conv, w_ffn_down, g_post_ffn):
    batch, seq, d = x_prompt.shape
    dec_batch, dec_seq, _ = x_sample.shape
    depth = w_in.shape[0]
    past = cache_k.shape[2]
    kv_w = N_KV_HEADS * HEAD_DIM
    assert 1 + dec_batch <= MOD_ROWS

    cond = jnp.concatenate([c_ctx[None, :], c, jnp.zeros((MOD_ROWS - 1 - dec_batch, d), F32)], axis=0)

    def ctx_mod_map(i, tm):
        return 0

    def lat_mod_map(i, tm):
        return 1 + i // (dec_seq // tm)

    xp = x_prompt.reshape(batch * seq, d)
    xs = x_sample.reshape(dec_batch * dec_seq, d)
    new_k, new_v = [], []
    for l in range(depth):
        mixer_weights = (w_in[l].astype(BF16), w_sconv[l], w_attn_o[l].astype(BF16),
                         w_conv_o[l].astype(BF16), w_mix_out[l].astype(BF16), g_pre_mix[l][None, :])
        post_mix_gains = [g_post_mix[l][None, :], g_pre_ffn[l][None, :]]
        post_ffn_gains = [g_post_ffn[l][None, :]]
        mod = _modulation(cond, w_mod[l], b_mod[l]).reshape(MOD_ROWS, N_MOD, d)
        k_ctx = cache_k[:, l].reshape(dec_batch, past, kv_w)
        v_ctx = cache_v[:, l].reshape(dec_batch, past, kv_w)

        y_p, k_l, v_l, (w_up_bf,) = _mixer(
            xp, mod, ctx_mod_map, seq, mixer_weights, attn_sink[l], f"ctx{l}",
            merge_rider=_cast_rider(w_ffn_up[l]))
        y_s, _, _, (x1_p, h2_p) = _mixer(
            xs, mod, lat_mod_map, dec_seq, mixer_weights, attn_sink[l], f"lat{l}",
            latent_ctx=(k_ctx, v_ctx),
            merge_rider=_rowwise_rider(_post_mix_kernel, [y_p, xp], post_mix_gains, mod,
                                       ctx_mod_map, [F32, BF16]))
        x1_s, h2_s = _rowwise_call(_post_mix_kernel, [y_s, xs], post_mix_gains, mod, lat_mod_map,
                                   [F32, BF16], f"post_mix_lat{l}")

        f_p, (w_down_bf,) = _ffn_up(h2_p, w_up_bf, w_ffn_conv[l], seq, f"ffn_up_ctx{l}",
                                    _cast_rider(w_ffn_down[l]))
        y2_p = _linear(f_p, w_down_bf, 0, d, F32, TM_DOWN, TN_WIDE, f"ffn_down_ctx{l}", _resident)
        f_s = _ffn_up(h2_s, w_up_bf, w_ffn_conv[l], dec_seq, f"ffn_up_lat{l}")
        y2_s, (xp,) = _linear(
            f_s, w_down_bf, 0, d, F32, TM_DOWN, TN_NARROW, f"ffn_down_lat{l}", _resident,
            rider=_rowwise_rider(_post_ffn_kernel, [y2_p, x1_p], post_ffn_gains, mod, ctx_mod_map,
                                 [F32]))
        (xs,) = _rowwise_call(_post_ffn_kernel, [y2_s, x1_s], post_ffn_gains, mod, lat_mod_map,
                              [F32], f"post_ffn_lat{l}")
        new_k.append(k_l.reshape(batch, seq, N_KV_HEADS, HEAD_DIM))
        new_v.append(v_l.reshape(batch, seq, N_KV_HEADS, HEAD_DIM))
    return (xp.reshape(batch, seq, d), xs.reshape(dec_batch, dec_seq, d),
            jnp.stack(new_k, axis=1), jnp.stack(new_v, axis=1))
```

```python
import functools
import math
from typing import Callable, NamedTuple

import jax
import jax.numpy as jnp
from jax import lax
from jax.experimental import pallas as pl
from jax.experimental.pallas import tpu as pltpu

F32 = jnp.float32
BF16 = jnp.bfloat16

HEAD_DIM = 128
N_HEADS = 32
N_KV_HEADS = 8
Q_PER_KV = N_HEADS // N_KV_HEADS
WINDOW_BLOCK = 128
GRID_W = 64
ROPE_THETA = 10000.0
EPS = 1e-6
N_MOD = 6
NEG = -1e30
SM_SCALE = HEAD_DIM ** -0.5
LOG2_E = math.log2(math.e)

TM_PROJ = 2048
TM_MERGE = 1024
TM_QKV = 1024
TM_DOWN = 1024
DOT_ROWS = 512
ROPE_DOT_ROWS = 256
SINK_KEY_ROWS = 128
TN_LINEAR = 1024
TN_WIDE = 512
TN_NARROW = 256
TM_ROWWISE = 256
TN_MOD = 512
BF16_SUBLANES = 16
Y_DTYPE = BF16
MOD_ROWS = BF16_SUBLANES

VMEM_LIMIT_BYTES = 56 * 1024 * 1024


def _params(*semantics):
    return pltpu.CompilerParams(dimension_semantics=semantics,
                                vmem_limit_bytes=VMEM_LIMIT_BYTES)


def _dot(a, b):
    return jnp.dot(a, b, preferred_element_type=F32)


def _dot_nt(a, b):
    return lax.dot_general(a, b, (((1,), (1,)), ((), ())), preferred_element_type=F32)


def _resident(shape, index_map):
    return pl.BlockSpec(shape, index_map, pipeline_mode=pl.Buffered(1))


def _mod_kernel(cond_ref, w_ref, b_ref, o_ref):
    c = cond_ref[...]
    s = (c * jax.nn.sigmoid(c)).astype(BF16)
    o_ref[...] = _dot(s, w_ref[...].astype(BF16)) + b_ref[...]


def _modulation(cond, w_mod, b_mod):
    rows, d = cond.shape
    n = w_mod.shape[1]
    return pl.pallas_call(
        _mod_kernel,
        grid=(n // TN_MOD,),
        in_specs=[pl.BlockSpec((rows, d), lambda j: (0, 0)),
                  pl.BlockSpec((d, TN_MOD), lambda j: (0, j)),
                  pl.BlockSpec((1, TN_MOD), lambda j: (0, j))],
        out_specs=pl.BlockSpec((rows, TN_MOD), lambda j: (0, j)),
        out_shape=jax.ShapeDtypeStruct((rows, n), F32),
        compiler_params=_params("parallel"),
        name="modulation",
    )(cond, w_mod, b_mod.reshape(1, n))


def _rms(x, g):
    return x * lax.rsqrt(jnp.mean(x * x, axis=-1, keepdims=True) + EPS) * g


def _modulate(y, mod_ref, shift_idx, scale_idx):
    return y * (1 + mod_ref[0, scale_idx:scale_idx + 1, :]) + mod_ref[0, shift_idx:shift_idx + 1, :]


def _prenorm_kernel(x_ref, g_ref, mod_ref, o_ref):
    o_ref[...] = _modulate(_rms(x_ref[...], g_ref[...]), mod_ref, 0, 1).astype(o_ref.dtype)


def _post_mix_kernel(y_ref, x_ref, g_post_ref, g_pre_ref, mod_ref, x1_ref, h2_ref):
    x1 = x_ref[...] + mod_ref[0, 2:3, :] * _rms(y_ref[...].astype(F32), g_post_ref[...])
    x1_ref[...] = x1
    h2_ref[...] = _modulate(_rms(x1, g_pre_ref[...]), mod_ref, 3, 4).astype(h2_ref.dtype)


def _post_ffn_kernel(y_ref, x_ref, g_post_ref, mod_ref, o_ref):
    o_ref[...] = x_ref[...] + mod_ref[0, 5:6, :] * _rms(y_ref[...].astype(F32), g_post_ref[...])


def _rowwise_call(kernel, row_inputs, gains, mod, mod_map, out_dtypes, name):
    m, d = row_inputs[0].shape
    row_spec = pl.BlockSpec((TM_ROWWISE, d), lambda i: (i, 0))
    gain_spec = pl.BlockSpec((1, d), lambda i: (0, 0))
    mod_spec = pl.BlockSpec((1, N_MOD, d), lambda i: (mod_map(i, TM_ROWWISE), 0, 0))
    return pl.pallas_call(
        kernel,
        grid=(m // TM_ROWWISE,),
        in_specs=[row_spec] * len(row_inputs) + [gain_spec] * len(gains) + [mod_spec],
        out_specs=[row_spec] * len(out_dtypes),
        out_shape=[jax.ShapeDtypeStruct((m, d), dt) for dt in out_dtypes],
        compiler_params=_params("parallel"),
        name=name,
    )(*row_inputs, *gains, mod)


def _linear_kernel(h_ref, w_ref, o_ref):
    o_ref[...] = _dot(h_ref[...], w_ref[...]).astype(o_ref.dtype)


def _linear(h, w, col_block0, n_out, out_dtype, tm, tn, name, lhs_spec=pl.BlockSpec, rider=None):
    m, k = h.shape
    return _projection_call(
        _linear_kernel, (m // tm, n_out // tn),
        [lhs_spec((tm, k), lambda i, j: (i, 0)),
         pl.BlockSpec((k, tn), lambda i, j: (0, col_block0 + j))],
        (h, w),
        pl.BlockSpec((tm, tn), lambda i, j: (i, j)), jax.ShapeDtypeStruct((m, n_out), out_dtype),
        name, rider)


def _dot_row_chunks(h_ref, w):
    rows = h_ref.shape[0]
    return jnp.concatenate([_dot(h_ref[r:r + DOT_ROWS, :], w) for r in range(0, rows, DOT_ROWS)],
                           axis=0)


def _merge_kernel(h_ref, a_ref, sc_ref, wga_ref, wgc_ref, wo_ref, wc_ref, o_ref):
    wga, wgc, wo, wc = wga_ref[...], wgc_ref[...], wo_ref[...], wc_ref[...]
    for r in range(0, h_ref.shape[0], DOT_ROWS):
        rows = slice(r, r + DOT_ROWS)
        h = h_ref[rows, :]
        att = jax.nn.sigmoid(_dot(h, wga)) * _dot(a_ref[rows, :], wo)
        conv = jax.nn.sigmoid(_dot(h, wgc)) * _dot(sc_ref[rows, :], wc)
        o_ref[rows, :] = (att + conv).astype(o_ref.dtype)


class Rider(NamedTuple):
    make: Callable


def _row_block(m, n_steps):
    rows = m // n_steps
    assert rows * n_steps == m and rows % BF16_SUBLANES == 0
    return rows


def _cast_kernel(w_ref, o_ref):
    o_ref[...] = w_ref[...].astype(o_ref.dtype)


def _cast_rider(w):
    def make(n_steps):
        block = ((_row_block(w.shape[0], n_steps), w.shape[1]), lambda s: (s, 0))
        return _cast_kernel, [(w, *block)], [(jax.ShapeDtypeStruct(w.shape, BF16), *block)]
    return Rider(make)


def _rowwise_rider(kernel, row_inputs, gains, mod, mod_map, out_dtypes):
    def make(n_steps):
        m, d = row_inputs[0].shape
        rows = _row_block(m, n_steps)
        block = ((rows, d), lambda s: (s, 0))
        inputs = ([(a, *block) for a in row_inputs]
                  + [(g, (1, d), lambda s: (0, 0)) for g in gains]
                  + [(mod, (1, N_MOD, d), lambda s: (mod_map(s, rows), 0, 0))])
        return kernel, inputs, [(jax.ShapeDtypeStruct((m, d), dt), *block) for dt in out_dtypes]
    return Rider(make)


def _projection_call(kernel, grid, in_specs, args, out_spec, out_shape, name, rider=None):
    if rider is None:
        return pl.pallas_call(kernel, grid=grid, in_specs=in_specs, out_specs=out_spec,
                              out_shape=out_shape, name=name,
                              compiler_params=_params("parallel", "arbitrary"))(*args)
    rider_kernel, r_in, r_out = rider.make(grid[0] * grid[1])
    n_in, n_rin, n_rout = len(args), len(r_in), len(r_out)

    def with_step(index_map):
        return lambda i, j: index_map(i * grid[1] + j)

    def hosted(*refs):
        rider_kernel(*refs[n_in:n_in + n_rin], *refs[n_in + n_rin + 1:])
        kernel(*refs[:n_in], refs[n_in + n_rin])

    outs = pl.pallas_call(
        hosted, grid=grid,
        in_specs=in_specs + [pl.BlockSpec(shape, with_step(f)) for _, shape, f in r_in],
        out_specs=[out_spec] + [pl.BlockSpec(shape, with_step(f)) for _, shape, f in r_out],
        out_shape=[out_shape] + [s for s, _, _ in r_out], name=name,
        compiler_params=_params("arbitrary", "arbitrary"))(*args, *[a for a, _, _ in r_in])
    return outs[0], outs[1:]


def _merge(h, a, sc, w_in, gate_col_blocks, w_attn_o, w_conv_o, name, rider=None):
    m, d = h.shape
    tm, tn = TM_MERGE, TN_NARROW
    ga0, gc0 = gate_col_blocks
    return _projection_call(
        _merge_kernel, (m // tm, d // tn),
        [_resident((tm, d), lambda i, j: (i, 0)),
         _resident((tm, a.shape[1]), lambda i, j: (i, 0)),
         _resident((tm, sc.shape[1]), lambda i, j: (i, 0)),
         pl.BlockSpec((d, tn), lambda i, j: (0, ga0 + j)),
         pl.BlockSpec((d, tn), lambda i, j: (0, gc0 + j)),
         pl.BlockSpec((a.shape[1], tn), lambda i, j: (0, j)),
         pl.BlockSpec((sc.shape[1], tn), lambda i, j: (0, j))],
        (h, a, sc, w_in, w_in, w_attn_o, w_conv_o),
        pl.BlockSpec((tm, tn), lambda i, j: (i, j)), jax.ShapeDtypeStruct((m, d), BF16),
        name, rider)


def _rope_kernel(h_ref, w_ref, cos_ref, sin_ref, o_ref):
    w = w_ref[...]
    lane = lax.broadcasted_iota(jnp.int32, (ROPE_DOT_ROWS, HEAD_DIM), 1)
    first_half = (lane % (HEAD_DIM // 2)) < (HEAD_DIM // 4)
    for r in range(0, h_ref.shape[0], ROPE_DOT_ROWS):
        rows = slice(r, r + ROPE_DOT_ROWS)
        y = _dot(h_ref[rows, :], w)
        cos = cos_ref[rows, :]
        sin = sin_ref[rows, :]
        for hd in range(y.shape[1] // HEAD_DIM):
            cols = slice(hd * HEAD_DIM, (hd + 1) * HEAD_DIM)
            x = y[:, cols]
            partner = jnp.where(first_half,
                                pltpu.roll(x, HEAD_DIM - HEAD_DIM // 4, axis=1),
                                pltpu.roll(x, HEAD_DIM // 4, axis=1))
            o_ref[rows, cols] = (x * cos + partner * sin).astype(o_ref.dtype)


def _rope_proj(h, w, col_block0, n_out, rope, tm, tn, name, rider=None):
    m, k = h.shape
    tiles_per_seq = rope[0].shape[0] // tm
    table_spec = pl.BlockSpec((tm, HEAD_DIM), lambda i, j: (i % tiles_per_seq, 0))
    return _projection_call(
        _rope_kernel, (m // tm, n_out // tn),
        [pl.BlockSpec((tm, k), lambda i, j: (i, 0)),
         pl.BlockSpec((k, tn), lambda i, j: (0, col_block0 + j)),
         table_spec, table_spec],
        (h, w, *rope),
        pl.BlockSpec((tm, tn), lambda i, j: (i, j)), jax.ShapeDtypeStruct((m, n_out), BF16),
        name, rider)


def _dwconv3_rows(x, w, seq_len):
    rows = x.shape[0]
    assert seq_len & (seq_len - 1) == 0 and rows % seq_len == 0
    t = lax.broadcasted_iota(jnp.int32, (rows, 1), 0) & (seq_len - 1)
    prev = jnp.where(t == 0, 0.0, pltpu.roll(x, 1, axis=0))
    nxt = jnp.where(t == seq_len - 1, 0.0, pltpu.roll(x, rows - 1, axis=0))
    return prev * w[0:1, :] + x * w[1:2, :] + nxt * w[2:3, :]


def _sconv_kernel(h_ref, wb_ref, wc_ref, wh_ref, cw_ref, o_ref, *, seq_len):
    conv = _dwconv3_rows(_dot_row_chunks(h_ref, wc_ref[...]) * _dot_row_chunks(h_ref, wh_ref[...]),
                         cw_ref[...], seq_len)
    o_ref[...] = (_dot_row_chunks(h_ref, wb_ref[...]) * conv).astype(o_ref.dtype)


def _short_conv(h, w_in, col_blocks, conv_w, seq_len, name):
    m, k = h.shape
    width = conv_w.shape[1]
    tm, tn = TM_PROJ, TN_NARROW
    w_specs = [pl.BlockSpec((k, tn), functools.partial(lambda i, j, c0: (0, c0 + j), c0=c0))
               for c0 in col_blocks]
    return pl.pallas_call(
        functools.partial(_sconv_kernel, seq_len=seq_len),
        grid=(m // tm, width // tn),
        in_specs=[_resident((tm, k), lambda i, j: (i, 0))] + w_specs
                 + [pl.BlockSpec((conv_w.shape[0], tn), lambda i, j: (0, j))],
        out_specs=pl.BlockSpec((tm, tn), lambda i, j: (i, j)),
        out_shape=jax.ShapeDtypeStruct((m, width), BF16),
        compiler_params=_params("parallel", "arbitrary"),
        name=name,
    )(h, w_in, w_in, w_in, conv_w)


def _ffn_up_kernel(h_ref, wg_ref, wv_ref, cw_ref, o_ref, *, seq_len):
    gate = _dwconv3_rows(_dot_row_chunks(h_ref, wg_ref[...]), cw_ref[...], seq_len)
    o_ref[...] = (gate * jax.nn.sigmoid(gate) * _dot_row_chunks(h_ref, wv_ref[...])).astype(o_ref.dtype)


def _ffn_up(h, w_up, conv_w, seq_len, name, rider=None):
    m, k = h.shape
    d_ff = conv_w.shape[1]
    tm, tn = TM_PROJ, TN_NARROW
    val0 = d_ff // tn
    return _projection_call(
        functools.partial(_ffn_up_kernel, seq_len=seq_len), (m // tm, d_ff // tn),
        [_resident((tm, k), lambda i, j: (i, 0)),
         pl.BlockSpec((k, tn), lambda i, j: (0, j)),
         pl.BlockSpec((k, tn), lambda i, j: (0, val0 + j)),
         pl.BlockSpec((conv_w.shape[0], tn), lambda i, j: (0, j))],
        (h, w_up, w_up, conv_w),
        pl.BlockSpec((tm, tn), lambda i, j: (i, j)), jax.ShapeDtypeStruct((m, d_ff), BF16),
        name, rider)


def _sink_attend_t(raw_scores_t, values, sink):
    n_q = sink.shape[1]
    raw_max = raw_scores_t[0].max(axis=0, keepdims=True)
    for s in raw_scores_t[1:]:
        raw_max = jnp.maximum(raw_max, s.max(axis=0, keepdims=True))
    m = jnp.maximum(raw_max * SM_SCALE, sink)
    m_log2 = m * LOG2_E
    probs_t = [jnp.exp2(s * (SM_SCALE * LOG2_E) - m_log2).astype(BF16) for s in raw_scores_t]
    sink_row = lax.broadcasted_iota(jnp.int32, (SINK_KEY_ROWS, n_q), 0) == 0
    probs_t.append(jnp.where(sink_row, jnp.exp(sink - m), 0.0).astype(BF16))
    p = jnp.concatenate(probs_t, axis=0).T
    v = jnp.concatenate(list(values) + [jnp.zeros((SINK_KEY_ROWS, HEAD_DIM), BF16)], axis=0)
    ones_col = (lax.broadcasted_iota(jnp.int32, v.shape, 1) == 0).astype(BF16)
    acc = _dot(p, jnp.concatenate([v, ones_col], axis=1))
    return acc[:, :HEAD_DIM] / acc[:, HEAD_DIM:HEAD_DIM + 1]


def _group_queries(q_ref, sink_ref, kv, rows):
    heads = [kv * Q_PER_KV + g for g in range(Q_PER_KV)]
    q = jnp.concatenate([q_ref[:, h * HEAD_DIM:(h + 1) * HEAD_DIM] for h in heads], axis=0)
    sink = jnp.concatenate([jnp.full((1, rows), sink_ref[h], F32) for h in heads], axis=1)
    return heads, q, sink


def _store_heads(o_ref, o, heads, rows):
    for g, h in enumerate(heads):
        o_ref[:, h * HEAD_DIM:(h + 1) * HEAD_DIM] = o[g * rows:(g + 1) * rows, :].astype(o_ref.dtype)


def _ctx_attn_kernel(sink_ref, q_ref, k_ref, v_ref, o_ref):
    rows = q_ref.shape[0]
    for kv in range(N_KV_HEADS):
        cols = slice(kv * HEAD_DIM, (kv + 1) * HEAD_DIM)
        heads, q, sink = _group_queries(q_ref, sink_ref, kv, rows)
        s_t = _dot_nt(k_ref[:, cols].astype(BF16), q)
        o = _sink_attend_t([s_t], [v_ref[:, cols].astype(BF16)], sink)
        _store_heads(o_ref, o, heads, rows)


def _context_attention(q, k, v, sink, seq_len, name):
    m = q.shape[0]
    return pl.pallas_call(
        _ctx_attn_kernel,
        grid=(m // seq_len,),
        in_specs=[pl.BlockSpec(memory_space=pltpu.SMEM),
                  pl.BlockSpec((seq_len, q.shape[1]), lambda b: (b, 0)),
                  pl.BlockSpec((seq_len, k.shape[1]), lambda b: (b, 0)),
                  pl.BlockSpec((seq_len, v.shape[1]), lambda b: (b, 0))],
        out_specs=pl.BlockSpec((seq_len, q.shape[1]), lambda b: (b, 0)),
        out_shape=jax.ShapeDtypeStruct(q.shape, BF16),
        compiler_params=_params("parallel"),
        name=name,
    )(sink, q, k, v)


def _lat_attn_kernel(sink_ref, q_ref, kp_ref, kc_ref, kn_ref, vp_ref, vc_ref, vn_ref,
                     kx_ref, vx_ref, o_ref, kx_bf, vx_bf):
    blk = WINDOW_BLOCK
    qb = pl.program_id(1)
    n_blocks = pl.num_programs(1)

    @pl.when(qb == 0)
    def _():
        kx_bf[...] = kx_ref[0].astype(BF16)
        vx_bf[...] = vx_ref[0].astype(BF16)

    shape = (3 * blk, Q_PER_KV * blk)
    c = lax.broadcasted_iota(jnp.int32, shape, 0)
    r = lax.broadcasted_iota(jnp.int32, shape, 1) & (blk - 1)
    in_window = (c >= r) & (c <= r + 2 * blk)
    in_sequence = ((c >= blk) | (qb > 0)) & ((c < 2 * blk) | (qb < n_blocks - 1))
    band_mask = in_window & in_sequence

    for kv in range(N_KV_HEADS):
        cols = slice(kv * HEAD_DIM, (kv + 1) * HEAD_DIM)
        heads, q, sink = _group_queries(q_ref, sink_ref, kv, blk)
        k_band = jnp.concatenate([kp_ref[:, cols], kc_ref[:, cols], kn_ref[:, cols]], axis=0)
        v_band = jnp.concatenate([vp_ref[:, cols], vc_ref[:, cols], vn_ref[:, cols]], axis=0)
        s_band_t = jnp.where(band_mask, _dot_nt(k_band, q), NEG)
        s_ctx_t = _dot_nt(kx_bf[:, cols], q)
        o = _sink_attend_t([s_band_t, s_ctx_t], [v_band, vx_bf[:, cols]], sink)
        _store_heads(o_ref, o, heads, blk)


def _latent_attention(q, k, v, k_ctx, v_ctx, sink, seq_len, name):
    m, q_w = q.shape
    kv_w = k.shape[1]
    n_batch, past, _ = k_ctx.shape
    blk = WINDOW_BLOCK
    nb = seq_len // blk

    def band_block(b, i, offset):
        return b * nb + jnp.clip(i + offset, 0, nb - 1)

    def band_spec(offset):
        return pl.BlockSpec((blk, kv_w), lambda b, i: (band_block(b, i, offset), 0))

    ctx_spec = pl.BlockSpec((1, past, kv_w), lambda b, i: (b, 0, 0))
    return pl.pallas_call(
        _lat_attn_kernel,
        grid=(n_batch, nb),
        in_specs=[pl.BlockSpec(memory_space=pltpu.SMEM),
                  pl.BlockSpec((blk, q_w), lambda b, i: (b * nb + i, 0)),
                  band_spec(-1), band_spec(0), band_spec(1),
                  band_spec(-1), band_spec(0), band_spec(1),
                  ctx_spec, ctx_spec],
        out_specs=pl.BlockSpec((blk, q_w), lambda b, i: (b * nb + i, 0)),
        out_shape=jax.ShapeDtypeStruct((m, q_w), BF16),
        scratch_shapes=[pltpu.VMEM((past, kv_w), BF16), pltpu.VMEM((past, kv_w), BF16)],
        compiler_params=_params("parallel", "arbitrary"),
        name=name,
    )(sink, q, k, k, k, v, v, v, k_ctx, v_ctx)


def _rope_tables(seq_len):
    rows = seq_len // GRID_W
    row = jnp.repeat(jnp.arange(rows), GRID_W).astype(F32)
    col = jnp.tile(jnp.arange(GRID_W), rows).astype(F32)
    n_freq = HEAD_DIM // 4
    inv = ROPE_THETA ** (-jnp.arange(n_freq, dtype=F32) / n_freq)
    cos_r, sin_r = jnp.cos(row[:, None] * inv), jnp.sin(row[:, None] * inv)
    cos_c, sin_c = jnp.cos(col[:, None] * inv), jnp.sin(col[:, None] * inv)
    cos = jnp.concatenate([cos_r, cos_r, cos_c, cos_c], axis=-1)
    sin = jnp.concatenate([-sin_r, sin_r, -sin_c, sin_c], axis=-1)
    return cos, sin


def _w_in_columns(d, conv_dim):
    k0 = N_HEADS * HEAD_DIM
    v0 = k0 + N_KV_HEADS * HEAD_DIM
    b0 = v0 + N_KV_HEADS * HEAD_DIM
    c0 = b0 + conv_dim
    h0 = c0 + conv_dim
    ga0 = h0 + conv_dim
    return k0, v0, b0, c0, h0, ga0, ga0 + d


def _mixer_branches(x, mod, mod_map, seq_len, weights, sink, tag, latent_ctx=None, q_rider=None):
    w_in, w_sconv, _, _, _, g_pre_mix = weights
    d = x.shape[1]
    q_w = N_HEADS * HEAD_DIM
    kv_w = N_KV_HEADS * HEAD_DIM
    latent = latent_ctx is not None
    rope = _rope_tables(seq_len) if latent else None
    k0, v0, b0, c0, h0, _, _ = _w_in_columns(d, w_sconv.shape[1])
    ridden = []

    (h,) = _rowwise_call(_prenorm_kernel, [x], [g_pre_mix], mod, mod_map, [BF16], f"prenorm_{tag}")
    if latent:
        q = _rope_proj(h, w_in, 0, q_w, rope, TM_QKV, TN_WIDE, f"q_proj_{tag}", q_rider)
        if q_rider is not None:
            q, ridden = q
        k = _rope_proj(h, w_in, k0 // TN_WIDE, kv_w, rope, TM_QKV, TN_WIDE, f"k_proj_{tag}")
        v = _linear(h, w_in, v0 // TN_LINEAR, kv_w, BF16, TM_QKV, TN_LINEAR, f"v_proj_{tag}")
        a = _latent_attention(q, k, v, latent_ctx[0], latent_ctx[1], sink, seq_len, f"attn_{tag}")
    else:
        q = _linear(h, w_in, 0, q_w, BF16, TM_QKV, TN_LINEAR, f"q_proj_{tag}")
        k = _linear(h, w_in, k0 // TN_LINEAR, kv_w, F32, TM_QKV, TN_LINEAR, f"k_proj_{tag}")
        v = _linear(h, w_in, v0 // TN_LINEAR, kv_w, F32, TM_QKV, TN_LINEAR, f"v_proj_{tag}")
        a = _context_attention(q, k, v, sink, seq_len, f"attn_{tag}")

    sc = _short_conv(h, w_in, [b0 // TN_NARROW, c0 // TN_NARROW, h0 // TN_NARROW],
                     w_sconv, seq_len, f"short_conv_{tag}")
    return h, a, sc, k, v, ridden


def _mixer_merge(h, a, sc, weights, tag, merge_rider):
    w_in, w_sconv, w_attn_o, w_conv_o, w_mix_out, _ = weights
    d = h.shape[1]
    ga0, gc0 = _w_in_columns(d, w_sconv.shape[1])[5:]
    merged, ridden = _merge(h, a, sc, w_in, (ga0 // TN_NARROW, gc0 // TN_NARROW), w_attn_o,
                            w_conv_o, f"merge_{tag}", merge_rider)
    y = _linear(merged, w_mix_out, 0, d, Y_DTYPE, TM_QKV, TN_LINEAR, f"mix_out_{tag}")
    return y, ridden


def kernel(x_prompt, x_sample, cache_k, cache_v, c, c_ctx, w_mod, b_mod, g_pre_mix, w_in, w_sconv, attn_sink, w_attn_o, w_conv_o, w_mix_out, g_post_mix, g_pre_ffn, w_ffn_up, w_ffn_conv, w_ffn_down, g_post_ffn):
    batch, seq, d = x_prompt.shape
    dec_batch, dec_seq, _ = x_sample.shape
    depth = w_in.shape[0]
    past = cache_k.shape[2]
    kv_w = N_KV_HEADS * HEAD_DIM
    assert 1 + dec_batch <= MOD_ROWS

    cond = jnp.concatenate([c_ctx[None, :], c, jnp.zeros((MOD_ROWS - 1 - dec_batch, d), F32)], axis=0)

    def ctx_mod_map(i, tm):
        return 0

    def lat_mod_map(i, tm):
        return 1 + i // (dec_seq // tm)

    xp = x_prompt.reshape(batch * seq, d)
    xs = x_sample.reshape(dec_batch * dec_seq, d)
    new_k, new_v = [], []
    for l in range(depth):
        mixer_weights = (w_in[l].astype(BF16), w_sconv[l], w_attn_o[l].astype(BF16),
                         w_conv_o[l].astype(BF16), w_mix_out[l].astype(BF16), g_pre_mix[l][None, :])
        post_mix_gains = [g_post_mix[l][None, :], g_pre_ffn[l][None, :]]
        post_ffn_gains = [g_post_ffn[l][None, :]]
        mod = _modulation(cond, w_mod[l], b_mod[l]).reshape(MOD_ROWS, N_MOD, d)
        k_ctx = cache_k[:, l].reshape(dec_batch, past, kv_w)
        v_ctx = cache_v[:, l].reshape(dec_batch, past, kv_w)

        h_p, a_p, sc_p, k_l, v_l, _ = _mixer_branches(
            xp, mod, ctx_mod_map, seq, mixer_weights, attn_sink[l], f"ctx{l}")
        y_p, (w_up_bf,) = _mixer_merge(h_p, a_p, sc_p, mixer_weights, f"ctx{l}",
                                       _cast_rider(w_ffn_up[l]))
        h_s, a_s, sc_s, _, _, _ = _mixer_branches(
            xs, mod, lat_mod_map, dec_seq, mixer_weights, attn_sink[l], f"lat{l}",
            latent_ctx=(k_ctx, v_ctx))
        y_s, (x1_p, h2_p) = _mixer_merge(
            h_s, a_s, sc_s, mixer_weights, f"lat{l}",
            _rowwise_rider(_post_mix_kernel, [y_p, xp], post_mix_gains, mod, ctx_mod_map,
                           [F32, BF16]))
        x1_s, h2_s = _rowwise_call(_post_mix_kernel, [y_s, xs], post_mix_gains, mod, lat_mod_map,
                                   [F32, BF16], f"post_mix_lat{l}")

        f_p, (w_down_bf,) = _ffn_up(h2_p, w_up_bf, w_ffn_conv[l], seq, f"ffn_up_ctx{l}",
                                    _cast_rider(w_ffn_down[l]))
        y2_p = _linear(f_p, w_down_bf, 0, d, Y_DTYPE, TM_DOWN, TN_WIDE, f"ffn_down_ctx{l}",
                       _resident)
        f_s = _ffn_up(h2_s, w_up_bf, w_ffn_conv[l], dec_seq, f"ffn_up_lat{l}")
        y2_s, (xp,) = _linear(
            f_s, w_down_bf, 0, d, Y_DTYPE, TM_DOWN, TN_NARROW, f"ffn_down_lat{l}", _resident,
            rider=_rowwise_rider(_post_ffn_kernel, [y2_p, x1_p], post_ffn_gains, mod, ctx_mod_map,
                                 [F32]))
        (xs,) = _rowwise_call(_post_ffn_kernel, [y2_s, x1_s], post_ffn_gains, mod, lat_mod_map,
                              [F32], f"post_ffn_lat{l}")
        new_k.append(k_l.reshape(batch, seq, N_KV_HEADS, HEAD_DIM))
        new_v.append(v_l.reshape(batch, seq, N_KV_HEADS, HEAD_DIM))
    return (xp.reshape(batch, seq, d), xs.reshape(dec_batch, dec_seq, d),
            jnp.stack(new_k, axis=1), jnp.stack(new_v, axis=1))
```
